```python
import math
import jax, jax.numpy as jnp
from jax import lax
import numpy as np


D_MODEL = 1024
BATCH = 8
SEQ = 2048
DEPTH = 1

DELTA_HEAD_DIM = 128
DELTA_WIDTH = D_MODEL // 2
N_DELTA_HEADS = DELTA_WIDTH // DELTA_HEAD_DIM
SHORT_CONV = 5
CHUNK = 64
POOL_WINDOWS = (2, 4, 8, 16)
N_POOL_GROUPS = len(POOL_WINDOWS)
POOL_WIDTH = D_MODEL - DELTA_WIDTH
POOL_GROUP_DIM = POOL_WIDTH // N_POOL_GROUPS
MIX_WIDTH = DELTA_WIDTH + POOL_WIDTH
IN_COLS = 4 * DELTA_WIDTH + 4 * N_DELTA_HEADS + POOL_WIDTH
N_EXPERTS = 16
EC_CAPACITY = 2
EXPERT_FF = 2 * D_MODEL
RMS_EPS = 1e-6

kernel_name = 'hybrid_deltanet_pool_ecmoe_encoder'


def _rmsnorm(x, w):
    xf = x.astype(jnp.float32)
    y = xf * lax.rsqrt(jnp.mean(xf * xf, axis=-1, keepdims=True) + RMS_EPS)
    return (y * w.astype(jnp.float32)).astype(x.dtype)


def _l2norm(x):
    xf = x.astype(jnp.float32)
    return xf * lax.rsqrt(jnp.sum(xf * xf, axis=-1, keepdims=True) + RMS_EPS)


def _centred_dwconv(x, w):
    K = w.shape[0]
    return lax.conv_general_dilated(
        x, w[:, None, :].astype(x.dtype), window_strides=(1,),
        padding=[(K // 2, K // 2)], dimension_numbers=('NWC', 'WIO', 'NWC'),
        feature_group_count=x.shape[-1])


def _chunk_gated_delta_rule(q, k, v, g, beta):
    out_dtype = v.dtype
    q, k, v, g, beta = (a.astype(jnp.float32) for a in (q, k, v, g, beta))
    B, H, T, Dk = q.shape
    Dv = v.shape[-1]
    N = T // CHUNK
    q = q.reshape(B, H, N, CHUNK, Dk)
    k = k.reshape(B, H, N, CHUNK, Dk)
    v = v.reshape(B, H, N, CHUNK, Dv)
    beta = beta.reshape(B, H, N, CHUNK)
    g = jnp.cumsum(g.reshape(B, H, N, CHUNK), axis=-1)
    incl = jnp.tril(jnp.ones((CHUNK, CHUNK), dtype=bool))
    strict = jnp.tril(jnp.ones((CHUNK, CHUNK), dtype=bool), -1)
    diff = g[..., :, None] - g[..., None, :]
    decay = jnp.where(incl, jnp.exp(jnp.where(incl, diff, 0.0)), 0.0)
    kk = jnp.einsum('bhncd,bhnsd->bhncs', k, k)
    a_mat = jnp.where(strict, beta[..., :, None] * kk * decay, 0.0) + jnp.eye(CHUNK, dtype=jnp.float32)
    rhs = jnp.concatenate([v * beta[..., None], k * (beta * jnp.exp(g))[..., None]], axis=-1)
    sol = lax.linalg.triangular_solve(a_mat, rhs, left_side=True, lower=True, unit_diagonal=True)
    u, w = sol[..., :Dv], sol[..., Dv:]
    qk = jnp.where(incl, jnp.einsum('bhncd,bhnsd->bhncs', q, k) * decay, 0.0)
    q_dec = q * jnp.exp(g)[..., None]
    g_last = g[..., -1]
    k_dec = k * jnp.exp(g_last[..., None] - g)[..., None]

    def step(S, xs):
        q_c, k_c, u_c, w_c, qk_c, gl_c = xs
        v_new = u_c - jnp.einsum('bhcd,bhde->bhce', w_c, S)
        o_c = jnp.einsum('bhcd,bhde->bhce', q_c, S) + jnp.einsum('bhcs,bhse->bhce', qk_c, v_new)
        S = S * jnp.exp(gl_c)[..., None, None] + jnp.einsum('bhcd,bhce->bhde', k_c, v_new)
        return S, o_c

    xs = tuple(jnp.moveaxis(a, 2, 0) for a in (q_dec, k_dec, u, w, qk, g_last))
    _, o = lax.scan(step, jnp.zeros((B, H, Dk, Dv), jnp.float32), xs)
    return jnp.moveaxis(o, 0, 2).reshape(B, H, T, Dv).astype(out_dtype)


def _log_decay(a, a_log, dt_bias):
    gl = -jnp.exp(a_log.astype(jnp.float32)) * jax.nn.softplus(a.astype(jnp.float32) + dt_bias.astype(jnp.float32))
    return jnp.swapaxes(gl, 1, 2)


def _centred_mean_pool(u, window):
    B, T, C = u.shape
    lo = window // 2
    hi = window - lo - 1
    csum = jnp.concatenate([jnp.zeros((B, 1, C), jnp.float32), jnp.cumsum(u.astype(jnp.float32), axis=1)], axis=1)
    t = jnp.arange(T)
    start = jnp.maximum(t - lo, 0)
    end = jnp.minimum(t + hi + 1, T)
    total = csum[:, end] - csum[:, start]
    count = (end - start).astype(jnp.float32)
    return (total / count[None, :, None]).astype(u.dtype)


def _hybrid_mixer(xn, w_in, conv_w, a_log_fwd, dt_bias_fwd, a_log_bwd, dt_bias_bwd,
                  head_norm_w, pool_w, pool_scale, w_out):
    B, T, _ = xn.shape
    H, Dh = N_DELTA_HEADS, DELTA_HEAD_DIM
    proj = jnp.einsum('btd,dc->btc', xn, w_in)
    qkv, z, ab, u = jnp.split(proj, [3 * DELTA_WIDTH, 4 * DELTA_WIDTH, 4 * DELTA_WIDTH + 4 * H], axis=-1)
    qkv = jax.nn.silu(_centred_dwconv(qkv, conv_w))
    q, k, v = jnp.split(qkv, 3, axis=-1)
    to_heads = lambda a: jnp.swapaxes(a.reshape(B, T, H, Dh), 1, 2)
    q = (_l2norm(to_heads(q)) * (Dh ** -0.5)).astype(xn.dtype)
    k = _l2norm(to_heads(k)).astype(xn.dtype)
    v = to_heads(v)
    a_f, b_f, a_b, b_b = jnp.split(ab, 4, axis=-1)
    g_f = _log_decay(a_f, a_log_fwd, dt_bias_fwd)
    g_b = _log_decay(a_b, a_log_bwd, dt_bias_bwd)
    beta_f = jnp.swapaxes(jax.nn.sigmoid(b_f.astype(jnp.float32)), 1, 2)
    beta_b = jnp.swapaxes(jax.nn.sigmoid(b_b.astype(jnp.float32)), 1, 2)
    o_fwd = _chunk_gated_delta_rule(q, k, v, g_f, beta_f)
    rev = lambda a: jnp.flip(a, axis=2)
    o_bwd = rev(_chunk_gated_delta_rule(rev(q), rev(k), rev(v), rev(g_b), rev(beta_b)))
    o = jnp.swapaxes(o_fwd + o_bwd, 1, 2)
    o = _rmsnorm(o, head_norm_w) * jax.nn.silu(z.reshape(B, T, H, Dh))
    o_delta = o.reshape(B, T, DELTA_WIDTH)
    ug = u.reshape(B, T, N_POOL_GROUPS, POOL_GROUP_DIM)
    pooled = jnp.stack([_centred_mean_pool(ug[:, :, i], w) for i, w in enumerate(POOL_WINDOWS)], axis=2)
    o_pool = jnp.einsum('btgc,gcd->btgd', pooled - ug, pool_w).reshape(B, T, POOL_WIDTH) * pool_scale
    return jnp.einsum('btc,cd->btd', jnp.concatenate([o_delta, o_pool], axis=-1), w_out)


def _expert_choice_ffn(xn, router_w, w_gate, w_up, w_down):
    B, T, D = xn.shape
    cap = EC_CAPACITY * T // N_EXPERTS
    probs = jax.nn.softmax(jnp.einsum('btd,de->bte', xn, router_w).astype(jnp.float32), axis=-1)
    gate, idx = lax.top_k(jnp.swapaxes(probs, 1, 2), cap)
    bidx = jnp.arange(B)[:, None, None]
    xg = xn[bidx, idx]
    h = jax.nn.silu(jnp.einsum('becd,edf->becf', xg, w_gate)) * jnp.einsum('becd,edf->becf', xg, w_up)
    y = jnp.einsum('becf,efd->becd', h, w_down) * gate[..., None].astype(xn.dtype)
    return jnp.zeros_like(xn).at[bidx, idx].add(y)


def setup_inputs(seed: int = 0) -> dict:
    key = jax.random.key(seed)
    ks = jax.random.split(key, 20)
    H = N_DELTA_HEADS
    nrm = lambda k, shape, fan_in: jax.random.normal(k, shape, jnp.float32) * (fan_in ** -0.5)
    gain = lambda k, shape: 1.0 + 0.02 * jax.random.normal(k, shape, jnp.float32)
    def dt_bias(k):
        dt = jnp.exp(jax.random.uniform(k, (DEPTH, H), jnp.float32, math.log(1e-3), math.log(1e-1)))
        return dt + jnp.log(-jnp.expm1(-dt))
    return {
        'x': jax.random.normal(ks[0], (BATCH, SEQ, D_MODEL), jnp.float32),
        'norm_mix_w': gain(ks[1], (DEPTH, D_MODEL)),
        'w_in': nrm(ks[2], (DEPTH, D_MODEL, IN_COLS), D_MODEL),
        'conv_w': nrm(ks[3], (DEPTH, SHORT_CONV, 3 * DELTA_WIDTH), SHORT_CONV),
        'a_log_fwd': jnp.log(jax.random.uniform(ks[4], (DEPTH, H), jnp.float32, 1.0, 16.0)),
        'dt_bias_fwd': dt_bias(ks[5]),
        'a_log_bwd': jnp.log(jax.random.uniform(ks[6], (DEPTH, H), jnp.float32, 1.0, 16.0)),
        'dt_bias_bwd': dt_bias(ks[7]),
        'head_norm_w': gain(ks[8], (DEPTH, DELTA_HEAD_DIM)),
        'pool_w': nrm(ks[9], (DEPTH, N_POOL_GROUPS, POOL_GROUP_DIM, POOL_GROUP_DIM), POOL_GROUP_DIM),
        'pool_scale': gain(ks[10], (DEPTH, POOL_WIDTH)),
        'w_out': nrm(ks[11], (DEPTH, MIX_WIDTH, D_MODEL), MIX_WIDTH),
        'norm_ffn_w': gain(ks[12], (DEPTH, D_MODEL)),
        'router_w': nrm(ks[13], (DEPTH, D_MODEL, N_EXPERTS), D_MODEL),
        'expert_w_gate': nrm(ks[14], (DEPTH, N_EXPERTS, D_MODEL, EXPERT_FF), D_MODEL),
        'expert_w_up': nrm(ks[15], (DEPTH, N_EXPERTS, D_MODEL, EXPERT_FF), D_MODEL),
        'expert_w_down': nrm(ks[16], (DEPTH, N_EXPERTS, EXPERT_FF, D_MODEL), EXPERT_FF),
        'norm_final_w': gain(ks[17], (D_MODEL,)),
    }


def reference(x, norm_mix_w, w_in, conv_w, a_log_fwd, dt_bias_fwd, a_log_bwd, dt_bias_bwd,
              head_norm_w, pool_w, pool_scale, w_out, norm_ffn_w, router_w,
              expert_w_gate, expert_w_up, expert_w_down, norm_final_w):
    h = x
    for i in range(DEPTH):
        h = h + _hybrid_mixer(_rmsnorm(h, norm_mix_w[i]), w_in[i], conv_w[i],
                              a_log_fwd[i], dt_bias_fwd[i], a_log_bwd[i], dt_bias_bwd[i],
                              head_norm_w[i], pool_w[i], pool_scale[i], w_out[i])
        h = h + _expert_choice_ffn(_rmsnorm(h, norm_ffn_w[i]), router_w[i],
                                   expert_w_gate[i], expert_w_up[i], expert_w_down[i])
    return _rmsnorm(h, norm_final_w)
```

```python
import functools

import jax
import jax.numpy as jnp
from jax import lax
from jax.experimental import pallas as pl
from jax.experimental.pallas import tpu as pltpu

F32 = jnp.float32
BF16 = jnp.bfloat16
I32 = jnp.int32

D_MODEL = 1024
N_HEADS = 4
HEAD_DIM = 128
DELTA_WIDTH = N_HEADS * HEAD_DIM
POOL_WINDOWS = (2, 4, 8, 16)
POOL_GROUP_DIM = 128
POOL_WIDTH = len(POOL_WINDOWS) * POOL_GROUP_DIM
SHORT_CONV = 5
N_EXPERTS = 16
EC_CAPACITY = 2
EXPERT_FF = 2 * D_MODEL
RMS_EPS = 1e-6

CHUNK = 128
PREP_UNROLL = 4
N_BISECT = 64
CONV_PAD = 8
POOL_PAD = 16
N_AB = 4 * N_HEADS
N_COLQ = 4
N_COLQ_PAD = 8
VMEM_LIMIT = 56 * 1024 * 1024


def _sigmoid(x):
    return 1.0 / (1.0 + jnp.exp(-x))


def _softplus(x):
    return jnp.maximum(x, 0.0) + jnp.log(1.0 + jnp.exp(-jnp.abs(x)))


def _dot(a, b):
    return jnp.dot(a, b, preferred_element_type=F32)


def _dot_nt(a, b):
    return lax.dot_general(a, b, (((1,), (1,)), ((), ())), preferred_element_type=F32)


def _dot_tn(a, b):
    return lax.dot_general(a, b, (((0,), (0,)), ((), ())), preferred_element_type=F32)


def _rms(x, w):
    return x * lax.rsqrt(jnp.mean(x * x, axis=-1, keepdims=True) + RMS_EPS) * w


def _params(sem):
    return pltpu.CompilerParams(dimension_semantics=sem, vmem_limit_bytes=VMEM_LIMIT)


def _inproj_kernel(x_ref, nw_ref, wqkv_ref, wz_ref, wu_ref, wabt_ref,
                   qkv_ref, z_ref, u_ref, abt_ref):
    xn = _rms(x_ref[...], nw_ref[...]).astype(BF16)
    qkv_ref[...] = _dot(xn, wqkv_ref[...])
    z_ref[...] = _dot(xn, wz_ref[...])
    u_ref[...] = _dot(xn, wu_ref[...])
    abt_ref[...] = _dot_nt(wabt_ref[...], xn)


def _inproj(x2, nw, wqkv, wz, wu, wabt, tm=512):
    m = x2.shape[0]
    full = lambda a: pl.BlockSpec(a.shape, lambda i: (0, 0))
    rows = lambda n: pl.BlockSpec((tm, n), lambda i: (i, 0))
    return pl.pallas_call(
        _inproj_kernel,
        grid=(m // tm,),
        in_specs=[rows(D_MODEL), full(nw), full(wqkv), full(wz), full(wu), full(wabt)],
        out_specs=[rows(3 * DELTA_WIDTH), rows(DELTA_WIDTH), rows(POOL_WIDTH),
                   pl.BlockSpec((N_AB, tm), lambda i: (0, i))],
        out_shape=[jax.ShapeDtypeStruct((m, 3 * DELTA_WIDTH), F32),
                   jax.ShapeDtypeStruct((m, DELTA_WIDTH), F32),
                   jax.ShapeDtypeStruct((m, POOL_WIDTH), F32),
                   jax.ShapeDtypeStruct((N_AB, m), F32)],
        compiler_params=_params(("arbitrary",)),
        name="inproj",
    )(x2, nw, wqkv, wz, wu, wabt)


def _delta_kernel(gp_ref, q_ref, k_ref, v_ref, z_ref, cwq_ref, cwk_ref, cwv_ref, abt_ref,
                  hnw_ref, out_ref,
                  xpad, qn, kn, vn, rowb, lvl, sm_s, sn_s, oq_s, o_s):
    seq = q_ref.shape[0]
    nch = seq // CHUNK
    h = pl.program_id(1)
    n_lvl = lvl.shape[0]

    zeros_pad = jnp.zeros((CONV_PAD, HEAD_DIM), F32)
    for i, (src, cw, dst) in enumerate(((q_ref, cwq_ref, qn), (k_ref, cwk_ref, kn), (v_ref, cwv_ref, vn))):
        xpad[i, 0:CONV_PAD, :] = zeros_pad
        xpad[i, CONV_PAD + seq:CONV_PAD + seq + CONV_PAD, :] = zeros_pad
        xpad[i, CONV_PAD:CONV_PAD + seq, :] = src[...]
        acc = None
        for j in range(SHORT_CONV):
            off = CONV_PAD + j - SHORT_CONV // 2
            term = xpad[i, off:off + seq, :] * cw[j:j + 1, :]
            acc = term if acc is None else acc + term
        act = acc * _sigmoid(acc)
        if i == 0:
            act = act * lax.rsqrt(jnp.sum(act * act, axis=-1, keepdims=True) + RMS_EPS) * (HEAD_DIM ** -0.5)
        elif i == 1:
            act = act * lax.rsqrt(jnp.sum(act * act, axis=-1, keepdims=True) + RMS_EPS)
        dst[...] = act

    ri = lax.broadcasted_iota(I32, (CHUNK, CHUNK), 0)
    ci = lax.broadcasted_iota(I32, (CHUNK, CHUNK), 1)
    for d in range(2):
        a_row = abt_ref[2 * d * N_HEADS + h]
        b_row = abt_ref[(2 * d + 1) * N_HEADS + h]
        a_log = jnp.full((1, CHUNK), gp_ref[2 * d, h], F32)
        g_row = -jnp.exp(a_log) * _softplus(a_row + gp_ref[2 * d + 1, h])
        tri = (ri <= ci) if d == 0 else (ri >= ci)
        gc_row = jnp.dot(g_row, tri.astype(F32), precision=lax.Precision.HIGHEST,
                         preferred_element_type=F32)
        g_last = gc_row[:, CHUNK - 1:CHUNK] if d == 0 else gc_row[:, 0:1]
        g_last = jnp.broadcast_to(g_last, (nch, CHUNK))
        rowb[d, 0] = gc_row
        rowb[d, 1] = g_last
        rowb[d, 2] = jnp.exp(g_last)
        rowb[d, 3] = _sigmoid(b_row)
    sel_r = lax.broadcasted_iota(I32, (2 * N_COLQ_PAD, N_COLQ * CHUNK), 0)
    sel_c = lax.broadcasted_iota(I32, (2 * N_COLQ_PAD, N_COLQ * CHUNK), 1)
    col_sel = ((sel_r & (N_COLQ_PAD - 1)) == (sel_c >> (CHUNK.bit_length() - 1))).astype(BF16)
    pad_rows = jnp.zeros((N_COLQ_PAD - N_COLQ, CHUNK), F32)

    for l in range(n_lvl):
        same_pair = (ri >> (l + 1)) == (ci >> (l + 1))
        other_half = (ri >> l) != (ci >> l)
        lvl[l] = (same_pair & other_half).astype(F32)
    eye = (ri == ci).astype(F32)

    def prep_body(i, carry):
        chains = []
        for j in range(PREP_UNROLL):
            c = i * PREP_UNROLL + j
            r0 = pl.multiple_of(c * CHUNK, CHUNK)
            q = qn[pl.ds(r0, CHUNK), :]
            k = kn[pl.ds(r0, CHUNK), :]
            v = vn[pl.ds(r0, CHUNK), :]
            kb = k.astype(BF16)
            kk = _dot_nt(kb, kb)
            qk = _dot_nt(q.astype(BF16), kb)
            rows = jnp.concatenate(
                [rowb[0, 0, pl.ds(c, 1), :], rowb[0, 3, pl.ds(c, 1), :],
                 rowb[1, 0, pl.ds(c, 1), :], rowb[1, 3, pl.ds(c, 1), :], pad_rows], axis=0)
            rows_hi = rows.astype(BF16).astype(F32)
            split = jnp.concatenate([rows_hi, rows - rows_hi], axis=0).astype(BF16)
            cols = _dot_tn(split, col_sel)
            for d in range(2):
                incl = (ri >= ci) if d == 0 else (ri <= ci)
                strict = (ri > ci) if d == 0 else (ri < ci)
                gcol = cols[:, 2 * d * CHUNK:(2 * d + 1) * CHUNK]
                grow = rowb[d, 0, pl.ds(c, 1), :]
                beta = cols[:, (2 * d + 1) * CHUNK:(2 * d + 2) * CHUNK]
                decay = jnp.where(incl, jnp.exp(jnp.where(incl, gcol - grow, 0.0)), 0.0)
                a_mat = jnp.where(strict, beta * kk * decay, 0.0)
                egc = jnp.exp(gcol)
                g_last = rowb[d, 1, pl.ds(c, 1), :]
                chains.append(dict(
                    c=c, d=d, r0=r0, a=a_mat,
                    rhs=jnp.concatenate([k * (beta * egc), v * beta], axis=1).astype(BF16),
                    qkm=jnp.where(incl, qk * decay, 0.0).astype(BF16),
                    kd=(k * jnp.exp(g_last - gcol)).astype(BF16),
                    qd=q * egc))
        t_inv = [eye - ch["a"] * lvl[0] for ch in chains]
        for l in range(1, n_lvl):
            t_b = [t.astype(BF16) for t in t_inv]
            xs = [_dot(tb, (ch["a"] * lvl[l]).astype(BF16)) for tb, ch in zip(t_b, chains)]
            ys = [_dot(x.astype(BF16), tb) for x, tb in zip(xs, t_b)]
            t_inv = [t - y for t, y in zip(t_inv, ys)]
        wus = [_dot(t.astype(BF16), ch["rhs"]).astype(BF16) for t, ch in zip(t_inv, chains)]
        kwus = [_dot_tn(ch["kd"], wu) for ch, wu in zip(chains, wus)]
        qwus = [_dot(ch["qkm"], wu) for ch, wu in zip(chains, wus)]
        for ch, kwu, qwu in zip(chains, kwus, qwus):
            c, d, r0 = ch["c"], ch["d"], ch["r0"]
            gamma = rowb[d, 2, pl.ds(c, 1), :]
            sm_s[d, c] = (eye * gamma - kwu[:, :HEAD_DIM]).astype(BF16)
            sn_s[d, c] = kwu[:, HEAD_DIM:]
            oq_s[d, pl.ds(r0, CHUNK), :] = (ch["qd"] - qwu[:, :HEAD_DIM]).astype(BF16)
            o_s[d, pl.ds(r0, CHUNK), :] = qwu[:, HEAD_DIM:]
        return carry

    lax.fori_loop(0, nch // PREP_UNROLL, prep_body, 0)

    def scan_step(i, states):
        new_states = []
        for d in range(2):
            c = i if d == 0 else nch - 1 - i
            r0 = pl.multiple_of(c * CHUNK, CHUNK)
            s_b = states[d].astype(BF16)
            o_s[d, pl.ds(r0, CHUNK), :] += _dot(oq_s[d, pl.ds(r0, CHUNK), :], s_b)
            new_states.append(sn_s[d, c] + _dot(sm_s[d, c], s_b))
        return tuple(new_states)

    zero_state = jnp.zeros((HEAD_DIM, HEAD_DIM), F32)
    lax.fori_loop(0, nch, scan_step, (zero_state, zero_state), unroll=2)

    o = o_s[0] + o_s[1]
    zg = z_ref[...]
    out_ref[...] = _rms(o, hnw_ref[...]) * (zg * _sigmoid(zg))


def _delta(gate_params, qkv, z, conv_w, abt, hnw):
    bsz, seq, _ = qkv.shape
    nch = seq // CHUNK
    n_lvl = CHUNK.bit_length() - 1
    head = lambda off: pl.BlockSpec((None, seq, HEAD_DIM), lambda b, h: (b, 0, off + h))
    cw = lambda off: pl.BlockSpec((SHORT_CONV, HEAD_DIM), lambda b, h: (0, off + h))
    return pl.pallas_call(
        _delta_kernel,
        grid=(bsz, N_HEADS),
        in_specs=[pl.BlockSpec(memory_space=pltpu.SMEM),
                  head(0), head(N_HEADS), head(2 * N_HEADS), head(0),
                  cw(0), cw(N_HEADS), cw(2 * N_HEADS),
                  pl.BlockSpec((N_AB, None, nch, CHUNK), lambda b, h: (0, b, 0, 0)),
                  pl.BlockSpec(hnw.shape, lambda b, h: (0, 0))],
        out_specs=head(0),
        out_shape=jax.ShapeDtypeStruct((bsz, seq, DELTA_WIDTH), F32),
        scratch_shapes=[
            pltpu.VMEM((3, seq + 2 * CONV_PAD, HEAD_DIM), F32),
            pltpu.VMEM((seq, HEAD_DIM), F32),
            pltpu.VMEM((seq, HEAD_DIM), F32),
            pltpu.VMEM((seq, HEAD_DIM), F32),
            pltpu.VMEM((2, N_COLQ, nch, CHUNK), F32),
            pltpu.VMEM((n_lvl, CHUNK, CHUNK), F32),
            pltpu.VMEM((2, nch, HEAD_DIM, HEAD_DIM), BF16),
            pltpu.VMEM((2, nch, HEAD_DIM, HEAD_DIM), F32),
            pltpu.VMEM((2, seq, HEAD_DIM), BF16),
            pltpu.VMEM((2, seq, HEAD_DIM), F32),
        ],
        compiler_params=_params(("arbitrary", "arbitrary")),
        name="delta",
    )(gate_params, qkv, qkv, qkv, z, conv_w, conv_w, conv_w, abt, hnw)


def _mixout_kernel(u_ref, od_ref, x_ref, pw_ref, ps_ref, wout_ref, nw_ref, rw_ref,
                   h_ref, xn_ref, p_ref, upad):
    seq = u_ref.shape[0]
    tm = od_ref.shape[0]
    i = pl.program_id(1)
    n_tiles = pl.num_programs(1)
    t0 = pl.multiple_of(i * tm, tm)

    upad[POOL_PAD:POOL_PAD + tm, :] = u_ref[pl.ds(t0, tm), :]
    before = u_ref[pl.ds(pl.multiple_of(jnp.maximum(t0 - POOL_PAD, 0), POOL_PAD), POOL_PAD), :]
    after = u_ref[pl.ds(pl.multiple_of(jnp.minimum(t0 + tm, seq - POOL_PAD), POOL_PAD), POOL_PAD), :]
    upad[0:POOL_PAD, :] = jnp.where(i > 0, before, 0.0)
    upad[POOL_PAD + tm:POOL_PAD + tm + POOL_PAD, :] = jnp.where(i < n_tiles - 1, after, 0.0)

    tglob = t0 + lax.broadcasted_iota(I32, (tm, 1), 0)
    pooled_out = []
    for g, window in enumerate(POOL_WINDOWS):
        lo = window // 2
        hi = window - lo - 1
        cols = slice(g * POOL_GROUP_DIM, (g + 1) * POOL_GROUP_DIM)
        total = None
        for dlt in range(-lo, hi + 1):
            term = upad[POOL_PAD + dlt:POOL_PAD + dlt + tm, cols]
            total = term if total is None else total + term
        count = (jnp.minimum(tglob + hi + 1, seq) - jnp.maximum(tglob - lo, 0)).astype(F32)
        diff = total / count - upad[POOL_PAD:POOL_PAD + tm, cols]
        pooled_out.append(_dot(diff.astype(BF16), pw_ref[g]))
    o_pool = jnp.concatenate(pooled_out, axis=1) * ps_ref[...]

    h = (x_ref[...]
         + _dot(od_ref[...].astype(BF16), wout_ref[0:DELTA_WIDTH, :])
         + _dot(o_pool.astype(BF16), wout_ref[DELTA_WIDTH:DELTA_WIDTH + POOL_WIDTH, :]))
    h_ref[...] = h
    xn = _rms(h, nw_ref[...]).astype(BF16)
    xn_ref[...] = xn
    logits = _dot(xn, rw_ref[...])
    e = jnp.exp(logits - jnp.max(logits, axis=-1, keepdims=True))
    p_ref[...] = e / jnp.sum(e, axis=-1, keepdims=True)


def _mixout(u, od, x, pw, ps, wout, nw, rw, tm=512):
    bsz, seq, _ = x.shape
    full = lambda a: pl.BlockSpec(a.shape, lambda b, i: (0,) * a.ndim)
    tile = lambda n: pl.BlockSpec((None, tm, n), lambda b, i: (b, i, 0))
    return pl.pallas_call(
        _mixout_kernel,
        grid=(bsz, seq // tm),
        in_specs=[pl.BlockSpec((None, seq, POOL_WIDTH), lambda b, i: (b, 0, 0)),
                  tile(DELTA_WIDTH), tile(D_MODEL), full(pw), full(ps), full(wout), full(nw), full(rw)],
        out_specs=[tile(D_MODEL), tile(D_MODEL), tile(N_EXPERTS)],
        out_shape=[jax.ShapeDtypeStruct((bsz, seq, D_MODEL), F32),
                   jax.ShapeDtypeStruct((bsz, seq, D_MODEL), BF16),
                   jax.ShapeDtypeStruct((bsz, seq, N_EXPERTS), F32)],
        scratch_shapes=[pltpu.VMEM((tm + 2 * POOL_PAD, POOL_WIDTH), F32)],
        compiler_params=_params(("arbitrary", "arbitrary")),
        name="mixout",
    )(u, od, x, pw, ps, wout, nw, rw)


def _prefix_count(mask_f32, tri_b):
    seq = mask_f32.shape[0]
    carry = jnp.zeros((1, mask_f32.shape[1]), F32)
    blocks = []
    for r in range(0, seq, CHUNK):
        part = _dot(tri_b, mask_f32[r:r + CHUNK, :].astype(BF16)) + carry
        blocks.append(part)
        carry = part[CHUNK - 1:CHUNK, :]
    return jnp.concatenate(blocks, axis=0)


def _route_kernel(p_ref, slot_ref, *, cap):
    p = p_ref[...]
    n_e = p.shape[1]

    def search(_, bounds):
        lo, hi = bounds
        mid = 0.5 * (lo + hi)
        enough = jnp.sum((p >= mid).astype(F32), axis=0, keepdims=True) >= cap
        return jnp.where(enough, mid, lo), jnp.where(enough, hi, mid)

    lo, hi = lax.fori_loop(0, N_BISECT, search,
                           (jnp.zeros((1, n_e), F32), jnp.full((1, n_e), 2.0, F32)))
    above = p >= hi
    tied = (p >= lo) & (p < hi)
    need = cap - jnp.sum(above.astype(F32), axis=0, keepdims=True)
    ri = lax.broadcasted_iota(I32, (CHUNK, CHUNK), 0)
    ci = lax.broadcasted_iota(I32, (CHUNK, CHUNK), 1)
    tri_b = (ri >= ci).astype(BF16)
    tied_f = tied.astype(F32)
    tied_rank = _prefix_count(tied_f, tri_b) - tied_f
    sel = above | (tied & (tied_rank < need))
    sel_f = sel.astype(F32)
    pos = _prefix_count(sel_f, tri_b)
    slot_ref[...] = jnp.where(sel, pos - 1.0, -1.0).astype(I32)


def _route(probs, cap):
    bsz, seq, n_e = probs.shape
    spec = pl.BlockSpec((None, seq, n_e), lambda b: (b, 0, 0))
    return pl.pallas_call(
        functools.partial(_route_kernel, cap=cap),
        grid=(bsz,),
        in_specs=[spec],
        out_specs=spec,
        out_shape=jax.ShapeDtypeStruct((bsz, seq, n_e), I32),
        compiler_params=_params(("arbitrary",)),
        name="route",
    )(probs)


def _expert_column(x, e):
    lane = lax.broadcasted_iota(I32, x.shape, 1)
    return jnp.sum(jnp.where(lane == e, x, 0.0), axis=1, keepdims=True)


def _one_hot_slots(slot_f32, e, cap):
    col = _expert_column(slot_f32, e)
    slots = lax.broadcasted_iota(I32, (1, cap), 1).astype(F32)
    return (col == slots).astype(BF16)


def _gather_kernel(slot_ref, xn_ref, xg_ref):
    cap = xg_ref.shape[0]
    onehot = _one_hot_slots(slot_ref[...].astype(F32), pl.program_id(1), cap)
    xg_ref[...] = _dot_tn(onehot, xn_ref[...]).astype(BF16)


def _gather(slot, xn, cap):
    bsz, seq, n_e = slot.shape
    return pl.pallas_call(
        _gather_kernel,
        grid=(bsz, n_e),
        in_specs=[pl.BlockSpec((None, seq, n_e), lambda b, e: (b, 0, 0)),
                  pl.BlockSpec((None, seq, D_MODEL), lambda b, e: (b, 0, 0))],
        out_specs=pl.BlockSpec((None, None, cap, D_MODEL), lambda b, e: (e, b, 0, 0)),
        out_shape=jax.ShapeDtypeStruct((n_e, bsz, cap, D_MODEL), BF16),
        compiler_params=_params(("arbitrary", "arbitrary")),
        name="gather",
    )(slot, xn)


def _ffn_kernel(xg_ref, wg_ref, wu_ref, wd_ref, y_ref, acc_ref, wgb, wub, wdb, *, tr):
    f = pl.program_id(1)
    rows = xg_ref.shape[0]
    wgb[...] = wg_ref[...].astype(BF16)
    wub[...] = wu_ref[...].astype(BF16)
    wdb[...] = wd_ref[...].astype(BF16)

    def row_block(r, carry):
        r0 = pl.multiple_of(r * tr, tr)
        x = xg_ref[pl.ds(r0, tr), :]
        gate = _dot(x, wgb[...])
        up = _dot(x, wub[...])
        hidden = (gate * _sigmoid(gate) * up).astype(BF16)
        contrib = _dot(hidden, wdb[...])

        @pl.when(f == 0)
        def _():
            acc_ref[pl.ds(r0, tr), :] = contrib

        @pl.when(f != 0)
        def _():
            acc_ref[pl.ds(r0, tr), :] += contrib

        return carry

    lax.fori_loop(0, rows // tr, row_block, 0)

    @pl.when(f == pl.num_programs(1) - 1)
    def _():
        y_ref[...] = acc_ref[...].astype(BF16)


def _ffn(xg, wg, wu, wd, tf=512, tr=512):
    n_e, rows, _ = xg.shape
    ff = wg.shape[2]
    return pl.pallas_call(
        functools.partial(_ffn_kernel, tr=tr),
        grid=(n_e, ff // tf),
        in_specs=[pl.BlockSpec((None, rows, D_MODEL), lambda e, f: (e, 0, 0)),
                  pl.BlockSpec((None, D_MODEL, tf), lambda e, f: (e, 0, f)),
                  pl.BlockSpec((None, D_MODEL, tf), lambda e, f: (e, 0, f)),
                  pl.BlockSpec((None, tf, D_MODEL), lambda e, f: (e, f, 0))],
        out_specs=pl.BlockSpec((None, rows, D_MODEL), lambda e, f: (e, 0, 0)),
        out_shape=jax.ShapeDtypeStruct((n_e, rows, D_MODEL), BF16),
        scratch_shapes=[pltpu.VMEM((rows, D_MODEL), F32),
                        pltpu.VMEM((D_MODEL, tf), BF16),
                        pltpu.VMEM((D_MODEL, tf), BF16),
                        pltpu.VMEM((tf, D_MODEL), BF16)],
        compiler_params=_params(("arbitrary", "arbitrary")),
        name="ffn",
    )(xg, wg, wu, wd)


def _combine_kernel(slot_ref, p_ref, y_ref, h_ref, nw_ref, out_ref):
    e = pl.program_id(2)
    cap = y_ref.shape[0]

    @pl.when(e == 0)
    def _():
        out_ref[...] = h_ref[...]

    onehot = _one_hot_slots(slot_ref[...].astype(F32), e, cap)
    gate = _expert_column(p_ref[...], e)
    out_ref[...] += _dot(onehot, y_ref[...]) * gate

    @pl.when(e == pl.num_programs(2) - 1)
    def _():
        out_ref[...] = _rms(out_ref[...], nw_ref[...])


def _combine(slot, probs, y, h, nw, tt=512):
    bsz, seq, n_e = slot.shape
    cap = y.shape[2]
    tok = lambda n: pl.BlockSpec((None, tt, n), lambda b, t, e: (b, t, 0))
    return pl.pallas_call(
        _combine_kernel,
        grid=(bsz, seq // tt, n_e),
        in_specs=[tok(n_e), tok(n_e),
                  pl.BlockSpec((None, None, cap, D_MODEL), lambda b, t, e: (e, b, 0, 0)),
                  tok(D_MODEL),
                  pl.BlockSpec(nw.shape, lambda b, t, e: (0, 0))],
        out_specs=tok(D_MODEL),
        out_shape=jax.ShapeDtypeStruct((bsz, seq, D_MODEL), F32),
        compiler_params=_params(("arbitrary", "arbitrary", "arbitrary")),
        name="combine",
    )(slot, probs, y, h, nw)


def _layer(x, norm_mix_w, w_in, conv_w, a_log_fwd, dt_bias_fwd, a_log_bwd, dt_bias_bwd,
           head_norm_w, pool_w, pool_scale, w_out, norm_ffn_w, router_w,
           expert_w_gate, expert_w_up, expert_w_down, norm_final_w):
    bsz, seq, _ = x.shape
    nch = seq // CHUNK
    cap = EC_CAPACITY * seq // N_EXPERTS
    c_z = 3 * DELTA_WIDTH
    c_ab = c_z + DELTA_WIDTH
    c_u = c_ab + N_AB
    w_bf = w_in.astype(BF16)
    qkv, z, u, abt = _inproj(
        x.reshape(bsz * seq, D_MODEL), norm_mix_w.reshape(1, D_MODEL),
        w_bf[:, :c_z], w_bf[:, c_z:c_ab], w_bf[:, c_u:], w_bf[:, c_ab:c_u].T)

    gate_params = jnp.stack([a_log_fwd, dt_bias_fwd, a_log_bwd, dt_bias_bwd]).astype(F32)
    o_delta = _delta(gate_params,
                     qkv.reshape(bsz, seq, c_z), z.reshape(bsz, seq, DELTA_WIDTH), conv_w,
                     abt.reshape(N_AB, bsz, nch, CHUNK), head_norm_w.reshape(1, HEAD_DIM))

    h, xn, probs = _mixout(u.reshape(bsz, seq, POOL_WIDTH), o_delta, x,
                           pool_w.astype(BF16), pool_scale.reshape(1, POOL_WIDTH),
                           w_out.astype(BF16), norm_ffn_w.reshape(1, D_MODEL), router_w.astype(BF16))
    slot = _route(probs, cap)
    xg = _gather(slot, xn, cap)
    y = _ffn(xg.reshape(N_EXPERTS, bsz * cap, D_MODEL), expert_w_gate, expert_w_up, expert_w_down)
    return _combine(slot, probs, y.reshape(N_EXPERTS, bsz, cap, D_MODEL), h,
                    norm_final_w.reshape(1, D_MODEL))


def kernel(x, norm_mix_w, w_in, conv_w, a_log_fwd, dt_bias_fwd, a_log_bwd, dt_bias_bwd,
           head_norm_w, pool_w, pool_scale, w_out, norm_ffn_w, router_w,
           expert_w_gate, expert_w_up, expert_w_down, norm_final_w):
    assert w_in.shape[0] == 1, "single-layer stack: the final norm is fused into the layer's last kernel"
    first = lambda a: a.reshape(a.shape[1:])
    return _layer(x, first(norm_mix_w), first(w_in), first(conv_w), first(a_log_fwd), first(dt_bias_fwd),
                  first(a_log_bwd), first(dt_bias_bwd), first(head_norm_w), first(pool_w),
                  first(pool_scale), first(w_out), first(norm_ffn_w), first(router_w),
                  first(expert_w_gate), first(expert_w_up), first(expert_w_down), norm_final_w)
```

```python
import functools

import jax
import jax.numpy as jnp
from jax import lax
from jax.experimental import pallas as pl
from jax.experimental.pallas import tpu as pltpu

F32 = jnp.float32
BF16 = jnp.bfloat16
I32 = jnp.int32

D_MODEL = 1024
N_HEADS = 4
HEAD_DIM = 128
DELTA_WIDTH = N_HEADS * HEAD_DIM
POOL_WINDOWS = (2, 4, 8, 16)
POOL_GROUP_DIM = 128
POOL_WIDTH = len(POOL_WINDOWS) * POOL_GROUP_DIM
SHORT_CONV = 5
N_EXPERTS = 16
EC_CAPACITY = 2
EXPERT_FF = 2 * D_MODEL
RMS_EPS = 1e-6

CHUNK = 128
PREP_UNROLL = 4
N_BISECT_GEO = 34
N_BISECT_LIN = 6
F32_TINY = 2.0 ** -126
CONV_PAD = 8
POOL_PAD = 16
N_AB = 4 * N_HEADS
N_COLQ = 4
N_COLQ_PAD = 8
VMEM_LIMIT = 56 * 1024 * 1024


def _sigmoid(x):
    return 1.0 / (1.0 + jnp.exp(-x))


def _softplus(x):
    return jnp.maximum(x, 0.0) + jnp.log(1.0 + jnp.exp(-jnp.abs(x)))


def _dot(a, b):
    return jnp.dot(a, b, preferred_element_type=F32)


def _dot_nt(a, b):
    return lax.dot_general(a, b, (((1,), (1,)), ((), ())), preferred_element_type=F32)


def _dot_tn(a, b):
    return lax.dot_general(a, b, (((0,), (0,)), ((), ())), preferred_element_type=F32)


def _rms(x, w):
    return x * lax.rsqrt(jnp.mean(x * x, axis=-1, keepdims=True) + RMS_EPS) * w


def _params(sem):
    return pltpu.CompilerParams(dimension_semantics=sem, vmem_limit_bytes=VMEM_LIMIT)


def _inproj_kernel(x_ref, nw_ref, wqkv_ref, wz_ref, wu_ref, wabt_ref,
                   qkv_ref, z_ref, u_ref, abt_ref):
    xn = _rms(x_ref[...], nw_ref[...]).astype(BF16)
    qkv_ref[...] = _dot(xn, wqkv_ref[...])
    z_ref[...] = _dot(xn, wz_ref[...])
    u_ref[...] = _dot(xn, wu_ref[...])
    abt_ref[...] = _dot_nt(wabt_ref[...], xn)


def _inproj(x2, nw, wqkv, wz, wu, wabt, tm=512):
    m = x2.shape[0]
    full = lambda a: pl.BlockSpec(a.shape, lambda i: (0, 0))
    rows = lambda n: pl.BlockSpec((tm, n), lambda i: (i, 0))
    return pl.pallas_call(
        _inproj_kernel,
        grid=(m // tm,),
        in_specs=[rows(D_MODEL), full(nw), full(wqkv), full(wz), full(wu), full(wabt)],
        out_specs=[rows(3 * DELTA_WIDTH), rows(DELTA_WIDTH), rows(POOL_WIDTH),
                   pl.BlockSpec((N_AB, tm), lambda i: (0, i))],
        out_shape=[jax.ShapeDtypeStruct((m, 3 * DELTA_WIDTH), F32),
                   jax.ShapeDtypeStruct((m, DELTA_WIDTH), F32),
                   jax.ShapeDtypeStruct((m, POOL_WIDTH), F32),
                   jax.ShapeDtypeStruct((N_AB, m), F32)],
        compiler_params=_params(("arbitrary",)),
        name="inproj",
    )(x2, nw, wqkv, wz, wu, wabt)


def _delta_kernel(gp_ref, q_ref, k_ref, v_ref, z_ref, cwq_ref, cwk_ref, cwv_ref, abt_ref,
                  hnw_ref, out_ref,
                  xpad, qn, kn, vn, rowb, lvl, sm_s, sn_s, oq_s, o_s):
    seq = q_ref.shape[0]
    nch = seq // CHUNK
    h = pl.program_id(1)
    n_lvl = lvl.shape[0]

    zeros_pad = jnp.zeros((CONV_PAD, HEAD_DIM), F32)
    for i, (src, cw, dst) in enumerate(((q_ref, cwq_ref, qn), (k_ref, cwk_ref, kn), (v_ref, cwv_ref, vn))):
        xpad[i, 0:CONV_PAD, :] = zeros_pad
        xpad[i, CONV_PAD + seq:CONV_PAD + seq + CONV_PAD, :] = zeros_pad
        xpad[i, CONV_PAD:CONV_PAD + seq, :] = src[...]
        acc = None
        for j in range(SHORT_CONV):
            off = CONV_PAD + j - SHORT_CONV // 2
            term = xpad[i, off:off + seq, :] * cw[j:j + 1, :]
            acc = term if acc is None else acc + term
        act = acc * _sigmoid(acc)
        if i == 0:
            act = act * lax.rsqrt(jnp.sum(act * act, axis=-1, keepdims=True) + RMS_EPS) * (HEAD_DIM ** -0.5)
        elif i == 1:
            act = act * lax.rsqrt(jnp.sum(act * act, axis=-1, keepdims=True) + RMS_EPS)
        dst[...] = act

    ri = lax.broadcasted_iota(I32, (CHUNK, CHUNK), 0)
    ci = lax.broadcasted_iota(I32, (CHUNK, CHUNK), 1)
    for d in range(2):
        a_row = abt_ref[2 * d * N_HEADS + h]
        b_row = abt_ref[(2 * d + 1) * N_HEADS + h]
        a_log = jnp.full((1, CHUNK), gp_ref[2 * d, h], F32)
        g_row = -jnp.exp(a_log) * _softplus(a_row + gp_ref[2 * d + 1, h])
        tri = (ri <= ci) if d == 0 else (ri >= ci)
        gc_row = jnp.dot(g_row, tri.astype(F32), precision=lax.Precision.HIGHEST,
                         preferred_element_type=F32)
        g_last = gc_row[:, CHUNK - 1:CHUNK] if d == 0 else gc_row[:, 0:1]
        g_last = jnp.broadcast_to(g_last, (nch, CHUNK))
        rowb[d, 0] = gc_row
        rowb[d, 1] = g_last
        rowb[d, 2] = jnp.exp(g_last)
        rowb[d, 3] = _sigmoid(b_row)
    sel_r = lax.broadcasted_iota(I32, (2 * N_COLQ_PAD, N_COLQ * CHUNK), 0)
    sel_c = lax.broadcasted_iota(I32, (2 * N_COLQ_PAD, N_COLQ * CHUNK), 1)
    col_sel = ((sel_r & (N_COLQ_PAD - 1)) == (sel_c >> (CHUNK.bit_length() - 1))).astype(BF16)
    pad_rows = jnp.zeros((N_COLQ_PAD - N_COLQ, CHUNK), F32)

    for l in range(n_lvl):
        same_pair = (ri >> (l + 1)) == (ci >> (l + 1))
        other_half = (ri >> l) != (ci >> l)
        lvl[l] = (same_pair & other_half).astype(F32)
    eye = (ri == ci).astype(F32)

    def prep_body(i, carry):
        chains = []
        for j in range(PREP_UNROLL):
            c = i * PREP_UNROLL + j
            r0 = pl.multiple_of(c * CHUNK, CHUNK)
            q = qn[pl.ds(r0, CHUNK), :]
            k = kn[pl.ds(r0, CHUNK), :]
            v = vn[pl.ds(r0, CHUNK), :]
            kb = k.astype(BF16)
            kk = _dot_nt(kb, kb)
            qk = _dot_nt(q.astype(BF16), kb)
            rows = jnp.concatenate(
                [rowb[0, 0, pl.ds(c, 1), :], rowb[0, 3, pl.ds(c, 1), :],
                 rowb[1, 0, pl.ds(c, 1), :], rowb[1, 3, pl.ds(c, 1), :], pad_rows], axis=0)
            rows_hi = rows.astype(BF16).astype(F32)
            split = jnp.concatenate([rows_hi, rows - rows_hi], axis=0).astype(BF16)
            cols = _dot_tn(split, col_sel)
            for d in range(2):
                incl = (ri >= ci) if d == 0 else (ri <= ci)
                strict = (ri > ci) if d == 0 else (ri < ci)
                gcol = cols[:, 2 * d * CHUNK:(2 * d + 1) * CHUNK]
                grow = rowb[d, 0, pl.ds(c, 1), :]
                beta = cols[:, (2 * d + 1) * CHUNK:(2 * d + 2) * CHUNK]
                decay = jnp.where(incl, jnp.exp(jnp.where(incl, gcol - grow, 0.0)), 0.0)
                a_mat = jnp.where(strict, beta * kk * decay, 0.0)
                egc = jnp.exp(gcol)
                g_last = rowb[d, 1, pl.ds(c, 1), :]
                chains.append(dict(
                    c=c, d=d, r0=r0, a=a_mat,
                    rhs=jnp.concatenate([k * (beta * egc), v * beta], axis=1).astype(BF16),
                    qkm=jnp.where(incl, qk * decay, 0.0).astype(BF16),
                    kd=(k * jnp.exp(g_last - gcol)).astype(BF16),
                    qd=q * egc))
        t_inv = [eye - ch["a"] * lvl[0] for ch in chains]
        for l in range(1, n_lvl):
            t_b = [t.astype(BF16) for t in t_inv]
            xs = [_dot(tb, (ch["a"] * lvl[l]).astype(BF16)) for tb, ch in zip(t_b, chains)]
            ys = [_dot(x.astype(BF16), tb) for x, tb in zip(xs, t_b)]
            t_inv = [t - y for t, y in zip(t_inv, ys)]
        wus = [_dot(t.astype(BF16), ch["rhs"]).astype(BF16) for t, ch in zip(t_inv, chains)]
        kwus = [_dot_tn(ch["kd"], wu) for ch, wu in zip(chains, wus)]
        qwus = [_dot(ch["qkm"], wu) for ch, wu in zip(chains, wus)]
        for ch, kwu, qwu in zip(chains, kwus, qwus):
            c, d, r0 = ch["c"], ch["d"], ch["r0"]
            gamma = rowb[d, 2, pl.ds(c, 1), :]
            sm_s[d, c] = (eye * gamma - kwu[:, :HEAD_DIM]).astype(BF16)
            sn_s[d, c] = kwu[:, HEAD_DIM:]
            oq_s[d, pl.ds(r0, CHUNK), :] = (ch["qd"] - qwu[:, :HEAD_DIM]).astype(BF16)
            o_s[d, pl.ds(r0, CHUNK), :] = qwu[:, HEAD_DIM:]
        return carry

    lax.fori_loop(0, nch // PREP_UNROLL, prep_body, 0)

    def scan_step(i, states):
        new_states = []
        for d in range(2):
            c = i if d == 0 else nch - 1 - i
            r0 = pl.multiple_of(c * CHUNK, CHUNK)
            s_b = states[d].astype(BF16)
            o_s[d, pl.ds(r0, CHUNK), :] += _dot(oq_s[d, pl.ds(r0, CHUNK), :], s_b)
            new_states.append(sn_s[d, c] + _dot(sm_s[d, c], s_b))
        return tuple(new_states)

    zero_state = jnp.zeros((HEAD_DIM, HEAD_DIM), F32)
    lax.fori_loop(0, nch, scan_step, (zero_state, zero_state), unroll=2)

    o = o_s[0] + o_s[1]
    zg = z_ref[...]
    out_ref[...] = _rms(o, hnw_ref[...]) * (zg * _sigmoid(zg))


def _delta(gate_params, qkv, z, conv_w, abt, hnw):
    bsz, seq, _ = qkv.shape
    nch = seq // CHUNK
    n_lvl = CHUNK.bit_length() - 1
    head = lambda off: pl.BlockSpec((None, seq, HEAD_DIM), lambda b, h: (b, 0, off + h))
    cw = lambda off: pl.BlockSpec((SHORT_CONV, HEAD_DIM), lambda b, h: (0, off + h))
    return pl.pallas_call(
        _delta_kernel,
        grid=(bsz, N_HEADS),
        in_specs=[pl.BlockSpec(memory_space=pltpu.SMEM),
                  head(0), head(N_HEADS), head(2 * N_HEADS), head(0),
                  cw(0), cw(N_HEADS), cw(2 * N_HEADS),
                  pl.BlockSpec((N_AB, None, nch, CHUNK), lambda b, h: (0, b, 0, 0)),
                  pl.BlockSpec(hnw.shape, lambda b, h: (0, 0))],
        out_specs=head(0),
        out_shape=jax.ShapeDtypeStruct((bsz, seq, DELTA_WIDTH), F32),
        scratch_shapes=[
            pltpu.VMEM((3, seq + 2 * CONV_PAD, HEAD_DIM), F32),
            pltpu.VMEM((seq, HEAD_DIM), F32),
            pltpu.VMEM((seq, HEAD_DIM), F32),
            pltpu.VMEM((seq, HEAD_DIM), F32),
            pltpu.VMEM((2, N_COLQ, nch, CHUNK), F32),
            pltpu.VMEM((n_lvl, CHUNK, CHUNK), F32),
            pltpu.VMEM((2, nch, HEAD_DIM, HEAD_DIM), BF16),
            pltpu.VMEM((2, nch, HEAD_DIM, HEAD_DIM), F32),
            pltpu.VMEM((2, seq, HEAD_DIM), BF16),
            pltpu.VMEM((2, seq, HEAD_DIM), F32),
        ],
        compiler_params=_params(("arbitrary", "arbitrary")),
        name="delta",
    )(gate_params, qkv, qkv, qkv, z, conv_w, conv_w, conv_w, abt, hnw)


def _mixout_kernel(u_ref, od_ref, x_ref, pw_ref, ps_ref, wout_ref, nw_ref, rw_ref, rwt_ref,
                   h_ref, xn_ref, p_ref, pt_ref, upad):
    seq = u_ref.shape[0]
    tm = od_ref.shape[0]
    i = pl.program_id(1)
    n_tiles = pl.num_programs(1)
    t0 = pl.multiple_of(i * tm, tm)

    upad[POOL_PAD:POOL_PAD + tm, :] = u_ref[pl.ds(t0, tm), :]
    before = u_ref[pl.ds(pl.multiple_of(jnp.maximum(t0 - POOL_PAD, 0), POOL_PAD), POOL_PAD), :]
    after = u_ref[pl.ds(pl.multiple_of(jnp.minimum(t0 + tm, seq - POOL_PAD), POOL_PAD), POOL_PAD), :]
    upad[0:POOL_PAD, :] = jnp.where(i > 0, before, 0.0)
    upad[POOL_PAD + tm:POOL_PAD + tm + POOL_PAD, :] = jnp.where(i < n_tiles - 1, after, 0.0)

    tglob = t0 + lax.broadcasted_iota(I32, (tm, 1), 0)
    pooled_out = []
    for g, window in enumerate(POOL_WINDOWS):
        lo = window // 2
        hi = window - lo - 1
        cols = slice(g * POOL_GROUP_DIM, (g + 1) * POOL_GROUP_DIM)
        total = None
        for dlt in range(-lo, hi + 1):
            term = upad[POOL_PAD + dlt:POOL_PAD + dlt + tm, cols]
            total = term if total is None else total + term
        count = (jnp.minimum(tglob + hi + 1, seq) - jnp.maximum(tglob - lo, 0)).astype(F32)
        diff = total / count - upad[POOL_PAD:POOL_PAD + tm, cols]
        pooled_out.append(_dot(diff.astype(BF16), pw_ref[g]))
    o_pool = jnp.concatenate(pooled_out, axis=1) * ps_ref[...]

    h = (x_ref[...]
         + _dot(od_ref[...].astype(BF16), wout_ref[0:DELTA_WIDTH, :])
         + _dot(o_pool.astype(BF16), wout_ref[DELTA_WIDTH:DELTA_WIDTH + POOL_WIDTH, :]))
    h_ref[...] = h
    xn = _rms(h, nw_ref[...]).astype(BF16)
    xn_ref[...] = xn
    logits = _dot(xn, rw_ref[...])
    e = jnp.exp(logits - jnp.max(logits, axis=-1, keepdims=True))
    p_ref[...] = e / jnp.sum(e, axis=-1, keepdims=True)
    logits_t = _dot_nt(rwt_ref[...], xn)
    e_t = jnp.exp(logits_t - jnp.max(logits_t, axis=0, keepdims=True))
    pt_ref[...] = e_t / jnp.sum(e_t, axis=0, keepdims=True)


def _mixout(u, od, x, pw, ps, wout, nw, rw, tm=512):
    bsz, seq, _ = x.shape
    rwt = rw.T
    full = lambda a: pl.BlockSpec(a.shape, lambda b, i: (0,) * a.ndim)
    tile = lambda n: pl.BlockSpec((None, tm, n), lambda b, i: (b, i, 0))
    return pl.pallas_call(
        _mixout_kernel,
        grid=(bsz, seq // tm),
        in_specs=[pl.BlockSpec((None, seq, POOL_WIDTH), lambda b, i: (b, 0, 0)),
                  tile(DELTA_WIDTH), tile(D_MODEL), full(pw), full(ps), full(wout), full(nw), full(rw),
                  full(rwt)],
        out_specs=[tile(D_MODEL), tile(D_MODEL), tile(N_EXPERTS),
                   pl.BlockSpec((None, N_EXPERTS, tm), lambda b, i: (b, 0, i))],
        out_shape=[jax.ShapeDtypeStruct((bsz, seq, D_MODEL), F32),
                   jax.ShapeDtypeStruct((bsz, seq, D_MODEL), BF16),
                   jax.ShapeDtypeStruct((bsz, seq, N_EXPERTS), F32),
                   jax.ShapeDtypeStruct((bsz, N_EXPERTS, seq), F32)],
        scratch_shapes=[pltpu.VMEM((tm + 2 * POOL_PAD, POOL_WIDTH), F32)],
        compiler_params=_params(("arbitrary", "arbitrary")),
        name="mixout",
    )(u, od, x, pw, ps, wout, nw, rw, rwt)


def _route_kernel(pt_ref, slott_ref, slot_ref, *, cap):
    p = pt_ref[...]
    rows, seq = p.shape
    n_e = slot_ref.shape[2]

    def bracket(mid_fn):
        def step(_, bounds):
            lo, hi = bounds
            mid = mid_fn(lo, hi)
            enough = jnp.sum((p >= mid).astype(F32), axis=1, keepdims=True) >= cap
            return jnp.where(enough, mid, lo), jnp.where(enough, hi, mid)
        return step

    bounds = (jnp.zeros((rows, 1), F32), jnp.full((rows, 1), 2.0, F32))
    bounds = lax.fori_loop(0, N_BISECT_GEO, bracket(lambda lo, hi: jnp.sqrt(jnp.maximum(lo, F32_TINY) * hi)),
                           bounds)
    lo, hi = lax.fori_loop(0, N_BISECT_LIN, bracket(lambda lo, hi: 0.5 * (lo + hi)), bounds)

    above = p >= hi
    tied = (p >= lo) & (p < hi)
    need = cap - jnp.sum(above.astype(F32), axis=1, keepdims=True)
    ri = lax.broadcasted_iota(I32, (CHUNK, CHUNK), 0)
    ci = lax.broadcasted_iota(I32, (CHUNK, CHUNK), 1)
    tri_b = (ri <= ci).astype(BF16)

    def prefix_count(mask):
        carry = jnp.zeros((rows, 1), F32)
        blocks = []
        for c0 in range(0, seq, CHUNK):
            part = _dot(mask[:, c0:c0 + CHUNK].astype(BF16), tri_b) + carry
            blocks.append(part)
            carry = part[:, CHUNK - 1:CHUNK]
        return jnp.concatenate(blocks, axis=1)

    tied_f = tied.astype(F32)
    sel = above | (tied & (prefix_count(tied_f) - tied_f < need))
    slot_t = jnp.where(sel, prefix_count(sel.astype(F32)) - 1.0, -1.0)
    slott_ref[...] = slot_t.astype(I32)
    eye_b = (lax.broadcasted_iota(I32, (n_e, n_e), 0) == lax.broadcasted_iota(I32, (n_e, n_e), 1)).astype(BF16)
    for b in range(rows // n_e):
        slot_ref[b] = _dot_tn(slot_t[b * n_e:(b + 1) * n_e, :].astype(BF16), eye_b).astype(I32)


def _route(probs_t, cap):
    bsz, n_e, seq = probs_t.shape
    rows = bsz * n_e
    slot_t, slot = pl.pallas_call(
        functools.partial(_route_kernel, cap=cap),
        grid=(1,),
        in_specs=[pl.BlockSpec((rows, seq), lambda i: (0, 0))],
        out_specs=[pl.BlockSpec((rows, seq), lambda i: (0, 0)),
                   pl.BlockSpec((bsz, seq, n_e), lambda i: (0, 0, 0))],
        out_shape=[jax.ShapeDtypeStruct((rows, seq), I32),
                   jax.ShapeDtypeStruct((bsz, seq, n_e), I32)],
        compiler_params=_params(("arbitrary",)),
        name="route",
    )(probs_t.reshape(rows, seq))
    return slot_t.reshape(bsz, n_e, seq), slot


def _gather_kernel(slott_ref, xn_ref, xg_ref):
    n_e, cap = xg_ref.shape[0], xg_ref.shape[1]
    slot_t = slott_ref[...].astype(F32)
    slots = lax.broadcasted_iota(I32, (cap, 1), 0).astype(F32)
    xn = xn_ref[...]
    for e in range(n_e):
        onehot_t = (slot_t[e:e + 1, :] == slots).astype(BF16)
        xg_ref[e] = _dot(onehot_t, xn).astype(BF16)


def _gather(slot_t, xn, cap):
    bsz, n_e, seq = slot_t.shape
    return pl.pallas_call(
        _gather_kernel,
        grid=(bsz,),
        in_specs=[pl.BlockSpec((None, n_e, seq), lambda b: (b, 0, 0)),
                  pl.BlockSpec((None, seq, D_MODEL), lambda b: (b, 0, 0))],
        out_specs=pl.BlockSpec((n_e, None, cap, D_MODEL), lambda b: (0, b, 0, 0)),
        out_shape=jax.ShapeDtypeStruct((n_e, bsz, cap, D_MODEL), BF16),
        compiler_params=_params(("arbitrary",)),
        name="gather",
    )(slot_t, xn)


def _ffn_kernel(xg_ref, wg_ref, wu_ref, wd_ref, y_ref, acc_ref, wgb, wub, wdb, *, tr):
    f = pl.program_id(1)
    n_blocks = xg_ref.shape[0] // tr

    @pl.when(f == 0)
    def _():
        acc_ref[...] = jnp.zeros(acc_ref.shape, F32)

    wgb[...] = wg_ref[...].astype(BF16)
    wub[...] = wu_ref[...].astype(BF16)

    def gate_up(r):
        x = xg_ref[r * tr:(r + 1) * tr, :]
        return _dot(x, wgb[...]), _dot(x, wub[...])

    nxt = gate_up(0)
    wdb[...] = wd_ref[...].astype(BF16)
    for r in range(n_blocks):
        gate, up = nxt
        if r + 1 < n_blocks:
            nxt = gate_up(r + 1)
        hidden = (gate * _sigmoid(gate) * up).astype(BF16)
        acc_ref[r * tr:(r + 1) * tr, :] += _dot(hidden, wdb[...])

    @pl.when(f == pl.num_programs(1) - 1)
    def _():
        y_ref[...] = acc_ref[...].astype(BF16)


def _ffn(xg, wg, wu, wd, tf=512, tr=512):
    n_e, rows, _ = xg.shape
    ff = wg.shape[2]
    return pl.pallas_call(
        functools.partial(_ffn_kernel, tr=tr),
        grid=(n_e, ff // tf),
        in_specs=[pl.BlockSpec((None, rows, D_MODEL), lambda e, f: (e, 0, 0)),
                  pl.BlockSpec((None, D_MODEL, tf), lambda e, f: (e, 0, f)),
                  pl.BlockSpec((None, D_MODEL, tf), lambda e, f: (e, 0, f)),
                  pl.BlockSpec((None, tf, D_MODEL), lambda e, f: (e, f, 0))],
        out_specs=pl.BlockSpec((None, rows, D_MODEL), lambda e, f: (e, 0, 0)),
        out_shape=jax.ShapeDtypeStruct((n_e, rows, D_MODEL), BF16),
        scratch_shapes=[pltpu.VMEM((rows, D_MODEL), F32),
                        pltpu.VMEM((D_MODEL, tf), BF16),
                        pltpu.VMEM((D_MODEL, tf), BF16),
                        pltpu.VMEM((tf, D_MODEL), BF16)],
        compiler_params=_params(("arbitrary", "arbitrary")),
        name="ffn",
    )(xg, wg, wu, wd)


def _combine_kernel(slot_ref, p_ref, y_ref, h_ref, nw_ref, out_ref):
    n_e, cap = y_ref.shape[0], y_ref.shape[1]
    slot = slot_ref[...].astype(F32)
    gates = p_ref[...]
    slots = lax.broadcasted_iota(I32, (1, cap), 1).astype(F32)
    acc = h_ref[...]
    for e in range(n_e):
        onehot = (slot[:, e:e + 1] == slots).astype(BF16)
        acc = acc + _dot(onehot, y_ref[e]) * gates[:, e:e + 1]
    out_ref[...] = _rms(acc, nw_ref[...])


def _combine(slot, probs, y, h, nw, tt=512):
    bsz, seq, n_e = slot.shape
    cap = y.shape[2]
    tok = lambda n: pl.BlockSpec((None, tt, n), lambda b, t: (b, t, 0))
    return pl.pallas_call(
        _combine_kernel,
        grid=(bsz, seq // tt),
        in_specs=[tok(n_e), tok(n_e),
                  pl.BlockSpec((n_e, None, cap, D_MODEL), lambda b, t: (0, b, 0, 0)),
                  tok(D_MODEL),
                  pl.BlockSpec(nw.shape, lambda b, t: (0, 0))],
        out_specs=tok(D_MODEL),
        out_shape=jax.ShapeDtypeStruct((bsz, seq, D_MODEL), F32),
        compiler_params=_params(("arbitrary", "arbitrary")),
        name="combine",
    )(slot, probs, y, h, nw)


def _layer(x, norm_mix_w, w_in, conv_w, a_log_fwd, dt_bias_fwd, a_log_bwd, dt_bias_bwd,
           head_norm_w, pool_w, pool_scale, w_out, norm_ffn_w, router_w,
           expert_w_gate, expert_w_up, expert_w_down, norm_final_w):
    bsz, seq, _ = x.shape
    nch = seq // CHUNK
    cap = EC_CAPACITY * seq // N_EXPERTS
    c_z = 3 * DELTA_WIDTH
    c_ab = c_z + DELTA_WIDTH
    c_u = c_ab + N_AB
    w_bf = w_in.astype(BF16)
    qkv, z, u, abt = _inproj(
        x.reshape(bsz * seq, D_MODEL), norm_mix_w.reshape(1, D_MODEL),
        w_bf[:, :c_z], w_bf[:, c_z:c_ab], w_bf[:, c_u:], w_bf[:, c_ab:c_u].T)

    gate_params = jnp.stack([a_log_fwd, dt_bias_fwd, a_log_bwd, dt_bias_bwd]).astype(F32)
    o_delta = _delta(gate_params,
                     qkv.reshape(bsz, seq, c_z), z.reshape(bsz, seq, DELTA_WIDTH), conv_w,
                     abt.reshape(N_AB, bsz, nch, CHUNK), head_norm_w.reshape(1, HEAD_DIM))

    h, xn, probs, probs_t = _mixout(u.reshape(bsz, seq, POOL_WIDTH), o_delta, x,
                                    pool_w.astype(BF16), pool_scale.reshape(1, POOL_WIDTH),
                                    w_out.astype(BF16), norm_ffn_w.reshape(1, D_MODEL),
                                    router_w.astype(BF16))
    slot_t, slot = _route(probs_t, cap)
    xg = _gather(slot_t, xn, cap)
    y = _ffn(xg.reshape(N_EXPERTS, bsz * cap, D_MODEL), expert_w_gate, expert_w_up, expert_w_down)
    return _combine(slot, probs, y.reshape(N_EXPERTS, bsz, cap, D_MODEL), h,
                    norm_final_w.reshape(1, D_MODEL))


def kernel(x, norm_mix_w, w_in, conv_w, a_log_fwd, dt_bias_fwd, a_log_bwd, dt_bias_bwd,
           head_norm_w, pool_w, pool_scale, w_out, norm_ffn_w, router_w,
           expert_w_gate, expert_w_up, expert_w_down, norm_final_w):
    assert w_in.shape[0] == 1, "single-layer stack: the final norm is fused into the layer's last kernel"
    first = lambda a: a.reshape(a.shape[1:])
    return _layer(x, first(norm_mix_w), first(w_in), first(conv_w), first(a_log_fwd), first(dt_bias_fwd),
                  first(a_log_bwd), first(dt_bias_bwd), first(head_norm_w), first(pool_w),
                  first(pool_scale), first(w_out), first(norm_ffn_w), first(router_w),
                  first(expert_w_gate), first(expert_w_up), first(expert_w_down), norm_final_w)
```

```python
import functools

import jax
import jax.numpy as jnp
from jax import lax
from jax.experimental import pallas as pl
from jax.experimental.pallas import tpu as pltpu

F32 = jnp.float32
BF16 = jnp.bfloat16
I32 = jnp.int32

D_MODEL = 1024
N_HEADS = 4
HEAD_DIM = 128
DELTA_WIDTH = N_HEADS * HEAD_DIM
POOL_WINDOWS = (2, 4, 8, 16)
POOL_GROUP_DIM = 128
POOL_WIDTH = len(POOL_WINDOWS) * POOL_GROUP_DIM
SHORT_CONV = 5
N_EXPERTS = 16
EC_CAPACITY = 2
EXPERT_FF = 2 * D_MODEL
RMS_EPS = 1e-6

SUBLANES = 8
CHUNK = 128
PREP_UNROLL = 16
N_BISECT_GEO = 34
N_BISECT_LIN = 6
F32_TINY = 2.0 ** -126
CONV_PAD = 8
POOL_PAD = 16
N_AB = 4 * N_HEADS
N_COLQ = 4
N_COLQ_PAD = 8
VMEM_LIMIT = 56 * 1024 * 1024


def _sigmoid(x):
    return 1.0 / (1.0 + jnp.exp(-x))


def _softplus(x):
    return jnp.maximum(x, 0.0) + jnp.log(1.0 + jnp.exp(-jnp.abs(x)))


def _dot(a, b):
    return jnp.dot(a, b, preferred_element_type=F32)


def _dot_nt(a, b):
    return lax.dot_general(a, b, (((1,), (1,)), ((), ())), preferred_element_type=F32)


def _dot_tn(a, b):
    return lax.dot_general(a, b, (((0,), (0,)), ((), ())), preferred_element_type=F32)


def _rms(x, w):
    return x * lax.rsqrt(jnp.mean(x * x, axis=-1, keepdims=True) + RMS_EPS) * w


def _params(sem):
    return pltpu.CompilerParams(dimension_semantics=sem, vmem_limit_bytes=VMEM_LIMIT)


def _inproj_kernel(x_ref, nw_ref, wqkv_ref, wz_ref, wu_ref, wabt_ref,
                   qkv_ref, z_ref, u_ref, abt_ref):
    xn = _rms(x_ref[...], nw_ref[...]).astype(BF16)
    qkv_ref[...] = _dot(xn, wqkv_ref[...]).astype(BF16)
    z_ref[...] = _dot(xn, wz_ref[...]).astype(BF16)
    u_ref[...] = _dot(xn, wu_ref[...]).astype(BF16)
    abt_ref[...] = _dot_nt(wabt_ref[...], xn)


def _inproj(x2, nw, wqkv, wz, wu, wabt, tm=512):
    m = x2.shape[0]
    full = lambda a: pl.BlockSpec(a.shape, lambda i: (0, 0))
    rows = lambda n: pl.BlockSpec((tm, n), lambda i: (i, 0))
    return pl.pallas_call(
        _inproj_kernel,
        grid=(m // tm,),
        in_specs=[rows(D_MODEL), full(nw), full(wqkv), full(wz), full(wu), full(wabt)],
        out_specs=[rows(3 * DELTA_WIDTH), rows(DELTA_WIDTH), rows(POOL_WIDTH),
                   pl.BlockSpec((N_AB, tm), lambda i: (0, i))],
        out_shape=[jax.ShapeDtypeStruct((m, 3 * DELTA_WIDTH), BF16),
                   jax.ShapeDtypeStruct((m, DELTA_WIDTH), BF16),
                   jax.ShapeDtypeStruct((m, POOL_WIDTH), BF16),
                   jax.ShapeDtypeStruct((N_AB, m), F32)],
        compiler_params=_params(("arbitrary",)),
        name="inproj",
    )(x2, nw, wqkv, wz, wu, wabt)


def _block_rows(t, size, parity):
    n = t.shape[0]
    return jnp.concatenate([t[r:r + size] for r in range(parity * size, n, 2 * size)], axis=0)


def _merge_block_rows(t, new_rows, size, parity):
    n = t.shape[0]
    parts = []
    for i, r in enumerate(range(0, n, 2 * size)):
        new = new_rows[i * size:(i + 1) * size]
        old = t[r + (1 - parity) * size:r + (2 - parity) * size]
        parts += [old, new] if parity else [new, old]
    return jnp.concatenate(parts, axis=0)


def _delta_kernel(gp_ref, q_ref, k_ref, v_ref, z_ref, cwq_ref, cwk_ref, cwv_ref, abt_ref,
                  hnw_ref, out_ref,
                  xpad, qn, kn, vn, rowb, lvl, sm_s, sn_s, oq_s, o_s):
    seq = q_ref.shape[0]
    nch = seq // CHUNK
    h = pl.program_id(1)
    n_lvl = lvl.shape[0]

    zeros_pad = jnp.zeros((CONV_PAD, HEAD_DIM), F32)
    for i, (src, cw, dst) in enumerate(((q_ref, cwq_ref, qn), (k_ref, cwk_ref, kn), (v_ref, cwv_ref, vn))):
        xpad[i, 0:CONV_PAD, :] = zeros_pad
        xpad[i, CONV_PAD + seq:CONV_PAD + seq + CONV_PAD, :] = zeros_pad
        xpad[i, CONV_PAD:CONV_PAD + seq, :] = src[...].astype(F32)
        acc = None
        for j in range(SHORT_CONV):
            off = CONV_PAD + j - SHORT_CONV // 2
            term = xpad[i, off:off + seq, :] * cw[j:j + 1, :]
            acc = term if acc is None else acc + term
        act = acc * _sigmoid(acc)
        if i == 0:
            act = act * lax.rsqrt(jnp.sum(act * act, axis=-1, keepdims=True) + RMS_EPS) * (HEAD_DIM ** -0.5)
        elif i == 1:
            act = act * lax.rsqrt(jnp.sum(act * act, axis=-1, keepdims=True) + RMS_EPS)
        dst[...] = act

    ri = lax.broadcasted_iota(I32, (CHUNK, CHUNK), 0)
    ci = lax.broadcasted_iota(I32, (CHUNK, CHUNK), 1)
    for d in range(2):
        a_row = abt_ref[2 * d * N_HEADS + h]
        b_row = abt_ref[(2 * d + 1) * N_HEADS + h]
        a_log = jnp.full((1, CHUNK), gp_ref[2 * d, h], F32)
        g_row = -jnp.exp(a_log) * _softplus(a_row + gp_ref[2 * d + 1, h])
        tri = (ri <= ci) if d == 0 else (ri >= ci)
        gc_row = jnp.dot(g_row, tri.astype(F32), precision=lax.Precision.HIGHEST,
                         preferred_element_type=F32)
        g_last = gc_row[:, CHUNK - 1:CHUNK] if d == 0 else gc_row[:, 0:1]
        g_last = jnp.broadcast_to(g_last, (nch, CHUNK))
        rowb[d, 0] = gc_row
        rowb[d, 1] = g_last
        rowb[d, 2] = jnp.exp(g_last)
        rowb[d, 3] = _sigmoid(b_row)
    sel_r = lax.broadcasted_iota(I32, (2 * N_COLQ_PAD, N_COLQ * CHUNK), 0)
    sel_c = lax.broadcasted_iota(I32, (2 * N_COLQ_PAD, N_COLQ * CHUNK), 1)
    col_sel = ((sel_r & (N_COLQ_PAD - 1)) == (sel_c >> (CHUNK.bit_length() - 1))).astype(BF16)
    pad_rows = jnp.zeros((N_COLQ_PAD - N_COLQ, CHUNK), F32)

    for l in range(n_lvl):
        same_pair = (ri >> (l + 1)) == (ci >> (l + 1))
        other_half = (ri >> l) != (ci >> l)
        lvl[l] = (same_pair & other_half).astype(F32)
    eye = (ri == ci).astype(F32)

    def prep_body(i, carry):
        chains = []
        for j in range(PREP_UNROLL):
            c = i * PREP_UNROLL + j
            r0 = pl.multiple_of(c * CHUNK, CHUNK)
            q = qn[pl.ds(r0, CHUNK), :]
            k = kn[pl.ds(r0, CHUNK), :]
            v = vn[pl.ds(r0, CHUNK), :]
            kb = k.astype(BF16)
            kk = _dot_nt(kb, kb)
            qk = _dot_nt(q.astype(BF16), kb)
            rows = jnp.concatenate(
                [rowb[0, 0, pl.ds(c, 1), :], rowb[0, 3, pl.ds(c, 1), :],
                 rowb[1, 0, pl.ds(c, 1), :], rowb[1, 3, pl.ds(c, 1), :], pad_rows], axis=0)
            rows_hi = rows.astype(BF16).astype(F32)
            split = jnp.concatenate([rows_hi, rows - rows_hi], axis=0).astype(BF16)
            cols = _dot_tn(split, col_sel)
            for d in range(2):
                incl = (ri >= ci) if d == 0 else (ri <= ci)
                strict = (ri > ci) if d == 0 else (ri < ci)
                gcol = cols[:, 2 * d * CHUNK:(2 * d + 1) * CHUNK]
                grow = rowb[d, 0, pl.ds(c, 1), :]
                beta = cols[:, (2 * d + 1) * CHUNK:(2 * d + 2) * CHUNK]
                decay = jnp.where(incl, jnp.exp(jnp.where(incl, gcol - grow, 0.0)), 0.0)
                a_mat = jnp.where(strict, beta * kk * decay, 0.0)
                egc = jnp.exp(gcol)
                g_last = rowb[d, 1, pl.ds(c, 1), :]
                chains.append(dict(
                    c=c, d=d, r0=r0, a=a_mat,
                    rhs=jnp.concatenate([k * (beta * egc), v * beta], axis=1).astype(BF16),
                    qkm=jnp.where(incl, qk * decay, 0.0).astype(BF16),
                    kd=(k * jnp.exp(g_last - gcol)).astype(BF16),
                    qd=q * egc))
        t_inv = [eye - ch["a"] * lvl[0] for ch in chains]
        for l in range(1, n_lvl):
            size = 1 << l
            t_b = [t.astype(BF16) for t in t_inv]
            c_b = [(ch["a"] * lvl[l]).astype(BF16) for ch in chains]
            if size < SUBLANES:
                xs = [_dot(tb, cb) for tb, cb in zip(t_b, c_b)]
                ys = [_dot(x.astype(BF16), tb) for x, tb in zip(xs, t_b)]
                t_inv = [t - y for t, y in zip(t_inv, ys)]
            else:
                halves = [_block_rows(t, size, 1 - ch["d"]) for t, ch in zip(t_inv, chains)]
                xs = [_dot(hf.astype(BF16), cb) for hf, cb in zip(halves, c_b)]
                ys = [_dot(x.astype(BF16), tb) for x, tb in zip(xs, t_b)]
                t_inv = [_merge_block_rows(t, hf - y, size, 1 - ch["d"])
                         for t, hf, y, ch in zip(t_inv, halves, ys, chains)]
        wus = [_dot(t.astype(BF16), ch["rhs"]).astype(BF16) for t, ch in zip(t_inv, chains)]
        kwus = [_dot_tn(ch["kd"], wu) for ch, wu in zip(chains, wus)]
        qwus = [_dot(ch["qkm"], wu) for ch, wu in zip(chains, wus)]
        for ch, kwu, qwu in zip(chains, kwus, qwus):
            c, d, r0 = ch["c"], ch["d"], ch["r0"]
            gamma = rowb[d, 2, pl.ds(c, 1), :]
            sm_s[d, c] = (eye * gamma - kwu[:, :HEAD_DIM]).astype(BF16)
            sn_s[d, c] = kwu[:, HEAD_DIM:]
            oq_s[d, pl.ds(r0, CHUNK), :] = (ch["qd"] - qwu[:, :HEAD_DIM]).astype(BF16)
            o_s[d, pl.ds(r0, CHUNK), :] = qwu[:, HEAD_DIM:]
        return carry

    lax.fori_loop(0, nch // PREP_UNROLL, prep_body, 0)

    def scan_step(i, states):
        new_states = []
        for d in range(2):
            c = i if d == 0 else nch - 1 - i
            r0 = pl.multiple_of(c * CHUNK, CHUNK)
            s_b = states[d].astype(BF16)
            o_s[d, pl.ds(r0, CHUNK), :] += _dot(oq_s[d, pl.ds(r0, CHUNK), :], s_b)
            new_states.append(sn_s[d, c] + _dot(sm_s[d, c], s_b))
        return tuple(new_states)

    zero_state = jnp.zeros((HEAD_DIM, HEAD_DIM), F32)
    lax.fori_loop(0, nch, scan_step, (zero_state, zero_state), unroll=2)

    o = o_s[0] + o_s[1]
    zg = z_ref[...].astype(F32)
    out_ref[...] = (_rms(o, hnw_ref[...]) * (zg * _sigmoid(zg))).astype(out_ref.dtype)


def _delta(gate_params, qkv, z, conv_w, abt, hnw):
    bsz, seq, _ = qkv.shape
    nch = seq // CHUNK
    n_lvl = CHUNK.bit_length() - 1
    head = lambda off: pl.BlockSpec((None, seq, HEAD_DIM), lambda b, h: (b, 0, off + h))
    cw = lambda off: pl.BlockSpec((SHORT_CONV, HEAD_DIM), lambda b, h: (0, off + h))
    return pl.pallas_call(
        _delta_kernel,
        grid=(bsz, N_HEADS),
        in_specs=[pl.BlockSpec(memory_space=pltpu.SMEM),
                  head(0), head(N_HEADS), head(2 * N_HEADS), head(0),
                  cw(0), cw(N_HEADS), cw(2 * N_HEADS),
                  pl.BlockSpec((N_AB, None, nch, CHUNK), lambda b, h: (0, b, 0, 0)),
                  pl.BlockSpec(hnw.shape, lambda b, h: (0, 0))],
        out_specs=head(0),
        out_shape=jax.ShapeDtypeStruct((bsz, seq, DELTA_WIDTH), BF16),
        scratch_shapes=[
            pltpu.VMEM((3, seq + 2 * CONV_PAD, HEAD_DIM), F32),
            pltpu.VMEM((seq, HEAD_DIM), F32),
            pltpu.VMEM((seq, HEAD_DIM), F32),
            pltpu.VMEM((seq, HEAD_DIM), F32),
            pltpu.VMEM((2, N_COLQ, nch, CHUNK), F32),
            pltpu.VMEM((n_lvl, CHUNK, CHUNK), F32),
            pltpu.VMEM((2, nch, HEAD_DIM, HEAD_DIM), BF16),
            pltpu.VMEM((2, nch, HEAD_DIM, HEAD_DIM), F32),
            pltpu.VMEM((2, seq, HEAD_DIM), BF16),
            pltpu.VMEM((2, seq, HEAD_DIM), F32),
        ],
        compiler_params=_params(("arbitrary", "arbitrary")),
        name="delta",
    )(gate_params, qkv, qkv, qkv, z, conv_w, conv_w, conv_w, abt, hnw)


def _mixout_kernel(u_ref, od_ref, x_ref, pw_ref, ps_ref, wout_ref, nw_ref, rw_ref, rwt_ref,
                   h_ref, xn_ref, p_ref, pt_ref, upad):
    seq = u_ref.shape[0]
    tm = od_ref.shape[0]
    i = pl.program_id(1)
    n_tiles = pl.num_programs(1)
    t0 = pl.multiple_of(i * tm, tm)

    upad[POOL_PAD:POOL_PAD + tm, :] = u_ref[pl.ds(t0, tm), :].astype(F32)
    before = u_ref[pl.ds(pl.multiple_of(jnp.maximum(t0 - POOL_PAD, 0), POOL_PAD), POOL_PAD), :].astype(F32)
    after = u_ref[pl.ds(pl.multiple_of(jnp.minimum(t0 + tm, seq - POOL_PAD), POOL_PAD), POOL_PAD), :].astype(F32)
    upad[0:POOL_PAD, :] = jnp.where(i > 0, before, 0.0)
    upad[POOL_PAD + tm:POOL_PAD + tm + POOL_PAD, :] = jnp.where(i < n_tiles - 1, after, 0.0)

    tglob = t0 + lax.broadcasted_iota(I32, (tm, 1), 0)
    pooled_out = []
    for g, window in enumerate(POOL_WINDOWS):
        lo = window // 2
        hi = window - lo - 1
        cols = slice(g * POOL_GROUP_DIM, (g + 1) * POOL_GROUP_DIM)
        total = None
        for dlt in range(-lo, hi + 1):
            term = upad[POOL_PAD + dlt:POOL_PAD + dlt + tm, cols]
            total = term if total is None else total + term
        count = (jnp.minimum(tglob + hi + 1, seq) - jnp.maximum(tglob - lo, 0)).astype(F32)
        diff = total / count - upad[POOL_PAD:POOL_PAD + tm, cols]
        pooled_out.append(_dot(diff.astype(BF16), pw_ref[g]))
    o_pool = jnp.concatenate(pooled_out, axis=1) * ps_ref[...]

    h = (x_ref[...]
         + _dot(od_ref[...], wout_ref[0:DELTA_WIDTH, :])
         + _dot(o_pool.astype(BF16), wout_ref[DELTA_WIDTH:DELTA_WIDTH + POOL_WIDTH, :]))
    h_ref[...] = h
    xn = _rms(h, nw_ref[...]).astype(BF16)
    xn_ref[...] = xn
    logits = _dot(xn, rw_ref[...])
    e = jnp.exp(logits - jnp.max(logits, axis=-1, keepdims=True))
    p_ref[...] = e / jnp.sum(e, axis=-1, keepdims=True)
    logits_t = _dot_nt(rwt_ref[...], xn)
    e_t = jnp.exp(logits_t - jnp.max(logits_t, axis=0, keepdims=True))
    pt_ref[...] = e_t / jnp.sum(e_t, axis=0, keepdims=True)


def _mixout(u, od, x, pw, ps, wout, nw, rw, tm=512):
    bsz, seq, _ = x.shape
    rwt = rw.T
    full = lambda a: pl.BlockSpec(a.shape, lambda b, i: (0,) * a.ndim)
    tile = lambda n: pl.BlockSpec((None, tm, n), lambda b, i: (b, i, 0))
    return pl.pallas_call(
        _mixout_kernel,
        grid=(bsz, seq // tm),
        in_specs=[pl.BlockSpec((None, seq, POOL_WIDTH), lambda b, i: (b, 0, 0)),
                  tile(DELTA_WIDTH), tile(D_MODEL), full(pw), full(ps), full(wout), full(nw), full(rw),
                  full(rwt)],
        out_specs=[tile(D_MODEL), tile(D_MODEL), tile(N_EXPERTS),
                   pl.BlockSpec((None, N_EXPERTS, tm), lambda b, i: (b, 0, i))],
        out_shape=[jax.ShapeDtypeStruct((bsz, seq, D_MODEL), F32),
                   jax.ShapeDtypeStruct((bsz, seq, D_MODEL), BF16),
                   jax.ShapeDtypeStruct((bsz, seq, N_EXPERTS), F32),
                   jax.ShapeDtypeStruct((bsz, N_EXPERTS, seq), F32)],
        scratch_shapes=[pltpu.VMEM((tm + 2 * POOL_PAD, POOL_WIDTH), F32)],
        compiler_params=_params(("arbitrary", "arbitrary")),
        name="mixout",
    )(u, od, x, pw, ps, wout, nw, rw, rwt)


def _route_kernel(pt_ref, slott_ref, slot_ref, *, cap):
    p = pt_ref[...]
    rows, seq = p.shape
    n_e = slot_ref.shape[2]

    def bracket(mid_fn):
        def step(_, bounds):
            lo, hi = bounds
            mid = mid_fn(lo, hi)
            enough = jnp.sum((p >= mid).astype(F32), axis=1, keepdims=True) >= cap
            return jnp.where(enough, mid, lo), jnp.where(enough, hi, mid)
        return step

    bounds = (jnp.zeros((rows, 1), F32), jnp.full((rows, 1), 2.0, F32))
    bounds = lax.fori_loop(0, N_BISECT_GEO, bracket(lambda lo, hi: jnp.sqrt(jnp.maximum(lo, F32_TINY) * hi)),
                           bounds)
    lo, hi = lax.fori_loop(0, N_BISECT_LIN, bracket(lambda lo, hi: 0.5 * (lo + hi)), bounds)

    above = p >= hi
    tied = (p >= lo) & (p < hi)
    need = cap - jnp.sum(above.astype(F32), axis=1, keepdims=True)
    ri = lax.broadcasted_iota(I32, (CHUNK, CHUNK), 0)
    ci = lax.broadcasted_iota(I32, (CHUNK, CHUNK), 1)
    tri_b = (ri <= ci).astype(BF16)

    def prefix_count(mask):
        carry = jnp.zeros((rows, 1), F32)
        blocks = []
        for c0 in range(0, seq, CHUNK):
            part = _dot(mask[:, c0:c0 + CHUNK].astype(BF16), tri_b) + carry
            blocks.append(part)
            carry = part[:, CHUNK - 1:CHUNK]
        return jnp.concatenate(blocks, axis=1)

    tied_f = tied.astype(F32)
    sel = above | (tied & (prefix_count(tied_f) - tied_f < need))
    slot_t = jnp.where(sel, prefix_count(sel.astype(F32)) - 1.0, -1.0)
    slott_ref[...] = slot_t.astype(I32)
    eye_b = (lax.broadcasted_iota(I32, (n_e, n_e), 0) == lax.broadcasted_iota(I32, (n_e, n_e), 1)).astype(BF16)
    for b in range(rows // n_e):
        slot_ref[b] = _dot_tn(slot_t[b * n_e:(b + 1) * n_e, :].astype(BF16), eye_b).astype(I32)


def _route(probs_t, cap):
    bsz, n_e, seq = probs_t.shape
    rows = bsz * n_e
    slot_t, slot = pl.pallas_call(
        functools.partial(_route_kernel, cap=cap),
        grid=(1,),
        in_specs=[pl.BlockSpec((rows, seq), lambda i: (0, 0))],
        out_specs=[pl.BlockSpec((rows, seq), lambda i: (0, 0)),
                   pl.BlockSpec((bsz, seq, n_e), lambda i: (0, 0, 0))],
        out_shape=[jax.ShapeDtypeStruct((rows, seq), I32),
                   jax.ShapeDtypeStruct((bsz, seq, n_e), I32)],
        compiler_params=_params(("arbitrary",)),
        name="route",
    )(probs_t.reshape(rows, seq))
    return slot_t.reshape(bsz, n_e, seq), slot


def _gather_kernel(slott_ref, xn_ref, xg_ref):
    n_e, cap = xg_ref.shape[0], xg_ref.shape[1]
    slot_t = slott_ref[...].astype(F32)
    slots = lax.broadcasted_iota(I32, (cap, 1), 0).astype(F32)
    xn = xn_ref[...]
    for e in range(n_e):
        onehot_t = (slot_t[e:e + 1, :] == slots).astype(BF16)
        xg_ref[e] = _dot(onehot_t, xn).astype(BF16)


def _gather(slot_t, xn, cap):
    bsz, n_e, seq = slot_t.shape
    return pl.pallas_call(
        _gather_kernel,
        grid=(bsz,),
        in_specs=[pl.BlockSpec((None, n_e, seq), lambda b: (b, 0, 0)),
                  pl.BlockSpec((None, seq, D_MODEL), lambda b: (b, 0, 0))],
        out_specs=pl.BlockSpec((n_e, None, cap, D_MODEL), lambda b: (0, b, 0, 0)),
        out_shape=jax.ShapeDtypeStruct((n_e, bsz, cap, D_MODEL), BF16),
        compiler_params=_params(("arbitrary",)),
        name="gather",
    )(slot_t, xn)


def _ffn_kernel(xg_ref, wg_ref, wu_ref, wd_ref, y_ref, acc_ref, wgb, wub, wdb, *, tr):
    f = pl.program_id(1)
    n_blocks = xg_ref.shape[0] // tr

    @pl.when(f == 0)
    def _():
        acc_ref[...] = jnp.zeros(acc_ref.shape, F32)

    def gate_up(r):
        x = xg_ref[r * tr:(r + 1) * tr, :]
        return _dot(x, wgb[...]), _dot(x, wub[...])

    wgb[...] = wg_ref[...].astype(BF16)
    gate0 = _dot(xg_ref[0:tr, :], wgb[...])
    wub[...] = wu_ref[...].astype(BF16)
    nxt = (gate0, _dot(xg_ref[0:tr, :], wub[...]))
    wdb[...] = wd_ref[...].astype(BF16)
    for r in range(n_blocks):
        gate, up = nxt
        if r + 1 < n_blocks:
            nxt = gate_up(r + 1)
        hidden = (gate * _sigmoid(gate) * up).astype(BF16)
        acc_ref[r * tr:(r + 1) * tr, :] += _dot(hidden, wdb[...])

    @pl.when(f == pl.num_programs(1) - 1)
    def _():
        y_ref[...] = acc_ref[...].astype(BF16)


def _ffn(xg, wg, wu, wd, tf=512, tr=512):
    n_e, rows, _ = xg.shape
    ff = wg.shape[2]
    return pl.pallas_call(
        functools.partial(_ffn_kernel, tr=tr),
        grid=(n_e, ff // tf),
        in_specs=[pl.BlockSpec((None, rows, D_MODEL), lambda e, f: (e, 0, 0)),
                  pl.BlockSpec((None, D_MODEL, tf), lambda e, f: (e, 0, f)),
                  pl.BlockSpec((None, D_MODEL, tf), lambda e, f: (e, 0, f)),
                  pl.BlockSpec((None, tf, D_MODEL), lambda e, f: (e, f, 0))],
        out_specs=pl.BlockSpec((None, rows, D_MODEL), lambda e, f: (e, 0, 0)),
        out_shape=jax.ShapeDtypeStruct((n_e, rows, D_MODEL), BF16),
        scratch_shapes=[pltpu.VMEM((rows, D_MODEL), F32),
                        pltpu.VMEM((D_MODEL, tf), BF16),
                        pltpu.VMEM((D_MODEL, tf), BF16),
                        pltpu.VMEM((tf, D_MODEL), BF16)],
        compiler_params=_params(("arbitrary", "arbitrary")),
        name="ffn",
    )(xg, wg, wu, wd)


def _combine_kernel(slot_ref, p_ref, y_ref, h_ref, nw_ref, out_ref):
    n_e, cap = y_ref.shape[0], y_ref.shape[1]
    slot = slot_ref[...].astype(F32)
    gates = p_ref[...]
    slots = lax.broadcasted_iota(I32, (1, cap), 1).astype(F32)
    acc = h_ref[...]
    for e in range(n_e):
        onehot = (slot[:, e:e + 1] == slots).astype(BF16)
        acc = acc + _dot(onehot, y_ref[e]) * gates[:, e:e + 1]
    out_ref[...] = _rms(acc, nw_ref[...])


def _combine(slot, probs, y, h, nw, tt=512):
    bsz, seq, n_e = slot.shape
    cap = y.shape[2]
    tok = lambda n: pl.BlockSpec((None, tt, n), lambda b, t: (b, t, 0))
    return pl.pallas_call(
        _combine_kernel,
        grid=(bsz, seq // tt),
        in_specs=[tok(n_e), tok(n_e),
                  pl.BlockSpec((n_e, None, cap, D_MODEL), lambda b, t: (0, b, 0, 0)),
                  tok(D_MODEL),
                  pl.BlockSpec(nw.shape, lambda b, t: (0, 0))],
        out_specs=tok(D_MODEL),
        out_shape=jax.ShapeDtypeStruct((bsz, seq, D_MODEL), F32),
        compiler_params=_params(("arbitrary", "arbitrary")),
        name="combine",
    )(slot, probs, y, h, nw)


def _layer(x, norm_mix_w, w_in, conv_w, a_log_fwd, dt_bias_fwd, a_log_bwd, dt_bias_bwd,
           head_norm_w, pool_w, pool_scale, w_out, norm_ffn_w, router_w,
           expert_w_gate, expert_w_up, expert_w_down, norm_final_w):
    bsz, seq, _ = x.shape
    nch = seq // CHUNK
    cap = EC_CAPACITY * seq // N_EXPERTS
    c_z = 3 * DELTA_WIDTH
    c_ab = c_z + DELTA_WIDTH
    c_u = c_ab + N_AB
    w_bf = w_in.astype(BF16)
    qkv, z, u, abt = _inproj(
        x.reshape(bsz * seq, D_MODEL), norm_mix_w.reshape(1, D_MODEL),
        w_bf[:, :c_z], w_bf[:, c_z:c_ab], w_bf[:, c_u:], w_bf[:, c_ab:c_u].T)

    gate_params = jnp.stack([a_log_fwd, dt_bias_fwd, a_log_bwd, dt_bias_bwd]).astype(F32)
    o_delta = _delta(gate_params,
                     qkv.reshape(bsz, seq, c_z), z.reshape(bsz, seq, DELTA_WIDTH), conv_w,
                     abt.reshape(N_AB, bsz, nch, CHUNK), head_norm_w.reshape(1, HEAD_DIM))

    h, xn, probs, probs_t = _mixout(u.reshape(bsz, seq, POOL_WIDTH), o_delta, x,
                                    pool_w.astype(BF16), pool_scale.reshape(1, POOL_WIDTH),
                                    w_out.astype(BF16), norm_ffn_w.reshape(1, D_MODEL),
                                    router_w.astype(BF16))
    slot_t, slot = _route(probs_t, cap)
    xg = _gather(slot_t, xn, cap)
    y = _ffn(xg.reshape(N_EXPERTS, bsz * cap, D_MODEL), expert_w_gate, expert_w_up, expert_w_down)
    return _combine(slot, probs, y.reshape(N_EXPERTS, bsz, cap, D_MODEL), h,
                    norm_final_w.reshape(1, D_MODEL))


def kernel(x, norm_mix_w, w_in, conv_w, a_log_fwd, dt_bias_fwd, a_log_bwd, dt_bias_bwd,
           head_norm_w, pool_w, pool_scale, w_out, norm_ffn_w, router_w,
           expert_w_gate, expert_w_up, expert_w_down, norm_final_w):
    assert w_in.shape[0] == 1, "single-layer stack: the final norm is fused into the layer's last kernel"
    first = lambda a: a.reshape(a.shape[1:])
    return _layer(x, first(norm_mix_w), first(w_in), first(conv_w), first(a_log_fwd), first(dt_bias_fwd),
                  first(a_log_bwd), first(dt_bias_bwd), first(head_norm_w), first(pool_w),
                  first(pool_scale), first(w_out), first(norm_ffn_w), first(router_w),
                  first(expert_w_gate), first(expert_w_up), first(expert_w_down), norm_final_w)
```

```python
import functools

import jax
import jax.numpy as jnp
from jax import lax
from jax.experimental import pallas as pl
from jax.experimental.pallas import tpu as pltpu

F32 = jnp.float32
BF16 = jnp.bfloat16
I32 = jnp.int32

D_MODEL = 1024
N_HEADS = 4
HEAD_DIM = 128
DELTA_WIDTH = N_HEADS * HEAD_DIM
POOL_WINDOWS = (2, 4, 8, 16)
POOL_GROUP_DIM = 128
POOL_WIDTH = len(POOL_WINDOWS) * POOL_GROUP_DIM
SHORT_CONV = 5
N_EXPERTS = 16
EC_CAPACITY = 2
EXPERT_FF = 2 * D_MODEL
RMS_EPS = 1e-6

MXU_COLS = 256
SUBLANES = 8
CHUNK = 128
N_BISECT_GEO = 34
N_BISECT_LIN = 6
F32_TINY = 2.0 ** -126
HALO = 16
POOL_PAD = 16
N_AB = 4 * N_HEADS
N_COLQ = 4
N_COLQ_PAD = 8
VMEM_LIMIT = 56 * 1024 * 1024


def _sigmoid(x):
    return 1.0 / (1.0 + jnp.exp(-x))


def _softplus(x):
    return jnp.maximum(x, 0.0) + jnp.log(1.0 + jnp.exp(-jnp.abs(x)))


def _dot(a, b):
    return jnp.dot(a, b, preferred_element_type=F32)


def _dot_nt(a, b):
    return lax.dot_general(a, b, (((1,), (1,)), ((), ())), preferred_element_type=F32)


def _dot_tn(a, b):
    return lax.dot_general(a, b, (((0,), (0,)), ((), ())), preferred_element_type=F32)


def _rms(x, w):
    return x * lax.rsqrt(jnp.mean(x * x, axis=-1, keepdims=True) + RMS_EPS) * w


def _params(sem):
    return pltpu.CompilerParams(dimension_semantics=sem, vmem_limit_bytes=VMEM_LIMIT)


def _inproj_kernel(x_ref, xprev_ref, xnext_ref, nw_ref, wqkv_ref, wz_ref, wu_ref, wabt_ref, cw_ref,
                   q_ref, k_ref, v_ref, z_ref, u_ref, abt_ref):
    tm = x_ref.shape[0]
    i = pl.program_id(1)
    nw = nw_ref[...]
    x_ext = jnp.concatenate([xprev_ref[...], x_ref[...], xnext_ref[...]], axis=0)
    xn_ext = _rms(x_ext, nw).astype(BF16)
    xn = xn_ext[HALO:HALO + tm, :]
    n_ext = tm + 2 * HALO
    keep_before = jnp.where(i > 0, 1.0, 0.0)
    keep_after = jnp.where(i < pl.num_programs(1) - 1, 1.0, 0.0)

    def project(pair):
        cols = slice(pair * MXU_COLS, (pair + 1) * MXU_COLS)
        res = _dot(xn_ext, wqkv_ref[:, cols])
        return jnp.concatenate([res[0:HALO] * keep_before, res[HALO:HALO + tm],
                                res[HALO + tm:n_ext] * keep_after], axis=0)

    def conv_act(blk, ext):
        cols = slice(blk * HEAD_DIM, (blk + 1) * HEAD_DIM)
        acc = None
        for j in range(SHORT_CONV):
            shift = (SHORT_CONV // 2 - j) % n_ext
            rolled = ext if shift == 0 else pltpu.roll(ext, shift, 0)
            term = rolled[HALO:HALO + tm] * cw_ref[j:j + 1, cols]
            acc = term if acc is None else acc + term
        act = acc * _sigmoid(acc)
        kind, head = divmod(blk, N_HEADS)
        if kind == 0:
            scale = lax.rsqrt(jnp.sum(act * act, axis=-1, keepdims=True) + RMS_EPS) * (HEAD_DIM ** -0.5)
            q_ref[head] = (act * scale).astype(BF16)
        elif kind == 1:
            scale = lax.rsqrt(jnp.sum(act * act, axis=-1, keepdims=True) + RMS_EPS)
            k_ref[head] = (act * scale).astype(BF16)
        else:
            v_ref[head] = act.astype(BF16)

    n_pairs = 3 * DELTA_WIDTH // MXU_COLS
    heads_per_pair = MXU_COLS // HEAD_DIM
    nxt = project(0)
    for pair in range(n_pairs):
        cur = nxt
        if pair + 1 < n_pairs:
            nxt = project(pair + 1)
        else:
            z = _dot(xn, wz_ref[...])
            for head in range(N_HEADS):
                z_ref[head] = z[:, head * HEAD_DIM:(head + 1) * HEAD_DIM].astype(BF16)
            u_ref[...] = _dot(xn, wu_ref[...]).astype(BF16)
            abt_ref[...] = _dot_nt(wabt_ref[...], xn)
        for sub in range(heads_per_pair):
            conv_act(pair * heads_per_pair + sub, cur[:, sub * HEAD_DIM:(sub + 1) * HEAD_DIM])


def _inproj(x, nw, wqkv, wz, wu, wabt, conv_w, tm=512):
    bsz, seq, _ = x.shape
    full = lambda a: pl.BlockSpec(a.shape, lambda b, i: (0, 0))
    heads = pl.BlockSpec((None, N_HEADS, tm, HEAD_DIM), lambda b, i: (b, 0, i, 0))
    halo_per_tile = tm // HALO
    head_shape = jax.ShapeDtypeStruct((bsz, N_HEADS, seq, HEAD_DIM), BF16)
    return pl.pallas_call(
        _inproj_kernel,
        grid=(bsz, seq // tm),
        in_specs=[pl.BlockSpec((None, tm, D_MODEL), lambda b, i: (b, i, 0)),
                  pl.BlockSpec((None, HALO, D_MODEL),
                               lambda b, i: (b, jnp.maximum(i * halo_per_tile - 1, 0), 0)),
                  pl.BlockSpec((None, HALO, D_MODEL),
                               lambda b, i: (b, jnp.minimum((i + 1) * halo_per_tile, seq // HALO - 1), 0)),
                  full(nw), full(wqkv), full(wz), full(wu), full(wabt), full(conv_w)],
        out_specs=[heads, heads, heads, heads,
                   pl.BlockSpec((None, tm, POOL_WIDTH), lambda b, i: (b, i, 0)),
                   pl.BlockSpec((None, N_AB, tm), lambda b, i: (b, 0, i))],
        out_shape=[head_shape, head_shape, head_shape, head_shape,
                   jax.ShapeDtypeStruct((bsz, seq, POOL_WIDTH), BF16),
                   jax.ShapeDtypeStruct((bsz, N_AB, seq), F32)],
        compiler_params=_params(("arbitrary", "arbitrary")),
        name="inproj",
    )(x, x, x, nw, wqkv, wz, wu, wabt, conv_w)


def _block_rows(t, size, parity):
    n = t.shape[0]
    return jnp.concatenate([t[r:r + size] for r in range(parity * size, n, 2 * size)], axis=0)


def _merge_block_rows(t, new_rows, size, parity):
    n = t.shape[0]
    parts = []
    for i, r in enumerate(range(0, n, 2 * size)):
        new = new_rows[i * size:(i + 1) * size]
        old = t[r + (1 - parity) * size:r + (2 - parity) * size]
        parts += [old, new] if parity else [new, old]
    return jnp.concatenate(parts, axis=0)


def _delta_kernel(gp_ref, q_ref, k_ref, v_ref, z_ref, abt_ref, hnw_ref, out_ref,
                  rowb, lvl, sm_s, sn_s, oq_s, o_s):
    n_heads, seq = q_ref.shape[0], q_ref.shape[1]
    nch = seq // CHUNK
    n_lvl = lvl.shape[0]
    ri = lax.broadcasted_iota(I32, (CHUNK, CHUNK), 0)
    ci = lax.broadcasted_iota(I32, (CHUNK, CHUNK), 1)

    def gate_rows(h):
        for d in range(2):
            a_row = abt_ref[2 * d * N_HEADS + h]
            b_row = abt_ref[(2 * d + 1) * N_HEADS + h]
            a_log = jnp.full((1, CHUNK), gp_ref[2 * d, h], F32)
            g_row = -jnp.exp(a_log) * _softplus(a_row + gp_ref[2 * d + 1, h])
            tri = (ri <= ci) if d == 0 else (ri >= ci)
            gc_row = jnp.dot(g_row, tri.astype(F32), precision=lax.Precision.HIGHEST,
                             preferred_element_type=F32)
            g_last = gc_row[:, CHUNK - 1:CHUNK] if d == 0 else gc_row[:, 0:1]
            g_last = jnp.broadcast_to(g_last, (nch, CHUNK))
            rowb[d, 0] = gc_row
            rowb[d, 1] = g_last
            rowb[d, 2] = jnp.exp(g_last)
            rowb[d, 3] = _sigmoid(b_row)

    sel_r = lax.broadcasted_iota(I32, (2 * N_COLQ_PAD, N_COLQ * CHUNK), 0)
    sel_c = lax.broadcasted_iota(I32, (2 * N_COLQ_PAD, N_COLQ * CHUNK), 1)
    col_sel = ((sel_r & (N_COLQ_PAD - 1)) == (sel_c >> (CHUNK.bit_length() - 1))).astype(BF16)
    pad_rows = jnp.zeros((N_COLQ_PAD - N_COLQ, CHUNK), F32)

    for l in range(n_lvl):
        same_pair = (ri >> (l + 1)) == (ci >> (l + 1))
        other_half = (ri >> l) != (ci >> l)
        lvl[l] = (same_pair & other_half).astype(F32)
    eye = (ri == ci).astype(F32)

    def prep_head(h, carry):
        gate_rows(h)
        chains = []
        for c in range(nch):
            r0 = c * CHUNK
            qb = q_ref[h, r0:r0 + CHUNK, :]
            kb = k_ref[h, r0:r0 + CHUNK, :]
            q = qb.astype(F32)
            k = kb.astype(F32)
            v = v_ref[h, r0:r0 + CHUNK, :].astype(F32)
            kk = _dot_nt(kb, kb)
            qk = _dot_nt(qb, kb)
            rows = jnp.concatenate(
                [rowb[0, 0, pl.ds(c, 1), :], rowb[0, 3, pl.ds(c, 1), :],
                 rowb[1, 0, pl.ds(c, 1), :], rowb[1, 3, pl.ds(c, 1), :], pad_rows], axis=0)
            rows_hi = rows.astype(BF16).astype(F32)
            split = jnp.concatenate([rows_hi, rows - rows_hi], axis=0).astype(BF16)
            cols = _dot_tn(split, col_sel)
            for d in range(2):
                incl = (ri >= ci) if d == 0 else (ri <= ci)
                strict = (ri > ci) if d == 0 else (ri < ci)
                gcol = cols[:, 2 * d * CHUNK:(2 * d + 1) * CHUNK]
                grow = rowb[d, 0, pl.ds(c, 1), :]
                beta = cols[:, (2 * d + 1) * CHUNK:(2 * d + 2) * CHUNK]
                decay = jnp.where(incl, jnp.exp(jnp.where(incl, gcol - grow, 0.0)), 0.0)
                a_mat = jnp.where(strict, beta * kk * decay, 0.0)
                egc = jnp.exp(gcol)
                g_last = rowb[d, 1, pl.ds(c, 1), :]
                chains.append(dict(
                    c=c, d=d, r0=r0, a=a_mat,
                    rhs=jnp.concatenate([k * (beta * egc), v * beta], axis=1).astype(BF16),
                    qkm=jnp.where(incl, qk * decay, 0.0).astype(BF16),
                    kd=(k * jnp.exp(g_last - gcol)).astype(BF16),
                    qd=q * egc))
        t_inv = [eye - ch["a"] * lvl[0] for ch in chains]
        for l in range(1, n_lvl):
            size = 1 << l
            t_b = [t.astype(BF16) for t in t_inv]
            c_b = [(ch["a"] * lvl[l]).astype(BF16) for ch in chains]
            if size < SUBLANES:
                xs = [_dot(tb, cb) for tb, cb in zip(t_b, c_b)]
                ys = [_dot(x.astype(BF16), tb) for x, tb in zip(xs, t_b)]
                t_inv = [t - y for t, y in zip(t_inv, ys)]
            else:
                halves = [_block_rows(t, size, 1 - ch["d"]) for t, ch in zip(t_inv, chains)]
                xs = [_dot(hf.astype(BF16), cb) for hf, cb in zip(halves, c_b)]
                ys = [_dot(x.astype(BF16), tb) for x, tb in zip(xs, t_b)]
                t_inv = [_merge_block_rows(t, hf - y, size, 1 - ch["d"])
                         for t, hf, y, ch in zip(t_inv, halves, ys, chains)]
        wus = [_dot(t.astype(BF16), ch["rhs"]).astype(BF16) for t, ch in zip(t_inv, chains)]
        kwus = [_dot_tn(ch["kd"], wu) for ch, wu in zip(chains, wus)]
        qwus = [_dot(ch["qkm"], wu) for ch, wu in zip(chains, wus)]
        for ch, kwu, qwu in zip(chains, kwus, qwus):
            c, d, r0 = ch["c"], ch["d"], ch["r0"]
            gamma = rowb[d, 2, pl.ds(c, 1), :]
            sm_s[h, d, c] = (eye * gamma - kwu[:, :HEAD_DIM]).astype(BF16)
            sn_s[h, d, c] = kwu[:, HEAD_DIM:].astype(BF16)
            oq_s[h, d, r0:r0 + CHUNK, :] = (ch["qd"] - qwu[:, :HEAD_DIM]).astype(BF16)
        for c in range(nch):
            o_s[h, c * CHUNK:(c + 1) * CHUNK, :] = qwus[2 * c][:, HEAD_DIM:] + qwus[2 * c + 1][:, HEAD_DIM:]
        return carry

    lax.fori_loop(0, n_heads, prep_head, 0)

    def scan_step(i, states):
        new_states = []
        for h in range(n_heads):
            for d in range(2):
                c = i if d == 0 else nch - 1 - i
                r0 = pl.multiple_of(c * CHUNK, CHUNK)
                s_b = states[2 * h + d].astype(BF16)
                o_s[h, pl.ds(r0, CHUNK), :] += _dot(oq_s[h, d, pl.ds(r0, CHUNK), :], s_b)
                new_states.append(sn_s[h, d, c].astype(F32) + _dot(sm_s[h, d, c], s_b))
        return tuple(new_states)

    zero_state = jnp.zeros((HEAD_DIM, HEAD_DIM), F32)
    lax.fori_loop(0, nch, scan_step, (zero_state,) * (2 * n_heads), unroll=2)

    for h in range(n_heads):
        o = o_s[h]
        zg = z_ref[h].astype(F32)
        out_ref[:, h * HEAD_DIM:(h + 1) * HEAD_DIM] = (
            _rms(o, hnw_ref[...]) * (zg * _sigmoid(zg))).astype(out_ref.dtype)


def _delta(gate_params, q, k, v, z, abt, hnw):
    bsz, n_heads, seq, _ = q.shape
    nch = seq // CHUNK
    n_lvl = CHUNK.bit_length() - 1
    heads = pl.BlockSpec((None, n_heads, seq, HEAD_DIM), lambda b: (b, 0, 0, 0))
    return pl.pallas_call(
        _delta_kernel,
        grid=(bsz,),
        in_specs=[pl.BlockSpec(memory_space=pltpu.SMEM),
                  heads, heads, heads, heads,
                  pl.BlockSpec((None, N_AB, nch, CHUNK), lambda b: (b, 0, 0, 0)),
                  pl.BlockSpec(hnw.shape, lambda b: (0, 0))],
        out_specs=pl.BlockSpec((None, seq, n_heads * HEAD_DIM), lambda b: (b, 0, 0)),
        out_shape=jax.ShapeDtypeStruct((bsz, seq, n_heads * HEAD_DIM), BF16),
        scratch_shapes=[
            pltpu.VMEM((2, N_COLQ, nch, CHUNK), F32),
            pltpu.VMEM((n_lvl, CHUNK, CHUNK), F32),
            pltpu.VMEM((n_heads, 2, nch, HEAD_DIM, HEAD_DIM), BF16),
            pltpu.VMEM((n_heads, 2, nch, HEAD_DIM, HEAD_DIM), BF16),
            pltpu.VMEM((n_heads, 2, seq, HEAD_DIM), BF16),
            pltpu.VMEM((n_heads, seq, HEAD_DIM), F32),
        ],
        compiler_params=_params(("arbitrary",)),
        name="delta",
    )(gate_params, q, k, v, z, abt, hnw)


def _mixout_kernel(u_ref, od_ref, x_ref, pw_ref, ps_ref, wout_ref, nw_ref, rw_ref, rwt_ref,
                   h_ref, xn_ref, p_ref, pt_ref, upad):
    seq = u_ref.shape[0]
    tm = od_ref.shape[0]
    i = pl.program_id(1)
    n_tiles = pl.num_programs(1)
    t0 = pl.multiple_of(i * tm, tm)

    upad[POOL_PAD:POOL_PAD + tm, :] = u_ref[pl.ds(t0, tm), :].astype(F32)
    before = u_ref[pl.ds(pl.multiple_of(jnp.maximum(t0 - POOL_PAD, 0), POOL_PAD), POOL_PAD), :].astype(F32)
    after = u_ref[pl.ds(pl.multiple_of(jnp.minimum(t0 + tm, seq - POOL_PAD), POOL_PAD), POOL_PAD), :].astype(F32)
    upad[0:POOL_PAD, :] = jnp.where(i > 0, before, 0.0)
    upad[POOL_PAD + tm:POOL_PAD + tm + POOL_PAD, :] = jnp.where(i < n_tiles - 1, after, 0.0)

    tglob = t0 + lax.broadcasted_iota(I32, (tm, 1), 0)
    pooled_out = []
    for g, window in enumerate(POOL_WINDOWS):
        lo = window // 2
        hi = window - lo - 1
        cols = slice(g * POOL_GROUP_DIM, (g + 1) * POOL_GROUP_DIM)
        total = None
        for dlt in range(-lo, hi + 1):
            term = upad[POOL_PAD + dlt:POOL_PAD + dlt + tm, cols]
            total = term if total is None else total + term
        count = (jnp.minimum(tglob + hi + 1, seq) - jnp.maximum(tglob - lo, 0)).astype(F32)
        diff = total / count - upad[POOL_PAD:POOL_PAD + tm, cols]
        pooled_out.append(_dot(diff.astype(BF16), pw_ref[g]))
    o_pool = jnp.concatenate(pooled_out, axis=1) * ps_ref[...]

    h = (x_ref[...]
         + _dot(od_ref[...], wout_ref[0:DELTA_WIDTH, :])
         + _dot(o_pool.astype(BF16), wout_ref[DELTA_WIDTH:DELTA_WIDTH + POOL_WIDTH, :]))
    h_ref[...] = h
    xn = _rms(h, nw_ref[...]).astype(BF16)
    xn_ref[...] = xn
    logits = _dot(xn, rw_ref[...])
    e = jnp.exp(logits - jnp.max(logits, axis=-1, keepdims=True))
    p_ref[...] = e / jnp.sum(e, axis=-1, keepdims=True)
    logits_t = _dot_nt(rwt_ref[...], xn)
    e_t = jnp.exp(logits_t - jnp.max(logits_t, axis=0, keepdims=True))
    pt_ref[...] = e_t / jnp.sum(e_t, axis=0, keepdims=True)


def _mixout(u, od, x, pw, ps, wout, nw, rw, tm=512):
    bsz, seq, _ = x.shape
    rwt = rw.T
    full = lambda a: pl.BlockSpec(a.shape, lambda b, i: (0,) * a.ndim)
    tile = lambda n: pl.BlockSpec((None, tm, n), lambda b, i: (b, i, 0))
    return pl.pallas_call(
        _mixout_kernel,
        grid=(bsz, seq // tm),
        in_specs=[pl.BlockSpec((None, seq, POOL_WIDTH), lambda b, i: (b, 0, 0)),
                  tile(DELTA_WIDTH), tile(D_MODEL), full(pw), full(ps), full(wout), full(nw), full(rw),
                  full(rwt)],
        out_specs=[tile(D_MODEL), tile(D_MODEL), tile(N_EXPERTS),
                   pl.BlockSpec((None, N_EXPERTS, tm), lambda b, i: (b, 0, i))],
        out_shape=[jax.ShapeDtypeStruct((bsz, seq, D_MODEL), F32),
                   jax.ShapeDtypeStruct((bsz, seq, D_MODEL), BF16),
                   jax.ShapeDtypeStruct((bsz, seq, N_EXPERTS), F32),
                   jax.ShapeDtypeStruct((bsz, N_EXPERTS, seq), F32)],
        scratch_shapes=[pltpu.VMEM((tm + 2 * POOL_PAD, POOL_WIDTH), F32)],
        compiler_params=_params(("arbitrary", "arbitrary")),
        name="mixout",
    )(u, od, x, pw, ps, wout, nw, rw, rwt)


def _route_kernel(pt_ref, slott_ref, slot_ref, *, cap):
    p = pt_ref[...]
    rows, seq = p.shape
    n_e = slot_ref.shape[2]

    def bracket(mid_fn):
        def step(_, bounds):
            lo, hi = bounds
            mid = mid_fn(lo, hi)
            enough = jnp.sum((p >= mid).astype(F32), axis=1, keepdims=True) >= cap
            return jnp.where(enough, mid, lo), jnp.where(enough, hi, mid)
        return step

    bounds = (jnp.zeros((rows, 1), F32), jnp.full((rows, 1), 2.0, F32))
    bounds = lax.fori_loop(0, N_BISECT_GEO, bracket(lambda lo, hi: jnp.sqrt(jnp.maximum(lo, F32_TINY) * hi)),
                           bounds)
    lo, hi = lax.fori_loop(0, N_BISECT_LIN, bracket(lambda lo, hi: 0.5 * (lo + hi)), bounds)

    above = p >= hi
    tied = (p >= lo) & (p < hi)
    need = cap - jnp.sum(above.astype(F32), axis=1, keepdims=True)
    ri = lax.broadcasted_iota(I32, (CHUNK, CHUNK), 0)
    ci = lax.broadcasted_iota(I32, (CHUNK, CHUNK), 1)
    tri_b = (ri <= ci).astype(BF16)

    def prefix_count(mask):
        carry = jnp.zeros((rows, 1), F32)
        blocks = []
        for c0 in range(0, seq, CHUNK):
            part = _dot(mask[:, c0:c0 + CHUNK].astype(BF16), tri_b) + carry
            blocks.append(part)
            carry = part[:, CHUNK - 1:CHUNK]
        return jnp.concatenate(blocks, axis=1)

    tied_f = tied.astype(F32)
    sel = above | (tied & (prefix_count(tied_f) - tied_f < need))
    slot_t = jnp.where(sel, prefix_count(sel.astype(F32)) - 1.0, -1.0)
    slott_ref[...] = slot_t.astype(I32)
    eye_b = (lax.broadcasted_iota(I32, (n_e, n_e), 0) == lax.broadcasted_iota(I32, (n_e, n_e), 1)).astype(BF16)
    for b in range(rows // n_e):
        slot_ref[b] = _dot_tn(slot_t[b * n_e:(b + 1) * n_e, :].astype(BF16), eye_b).astype(I32)


def _route(probs_t, cap):
    bsz, n_e, seq = probs_t.shape
    rows = bsz * n_e
    slot_t, slot = pl.pallas_call(
        functools.partial(_route_kernel, cap=cap),
        grid=(1,),
        in_specs=[pl.BlockSpec((rows, seq), lambda i: (0, 0))],
        out_specs=[pl.BlockSpec((rows, seq), lambda i: (0, 0)),
                   pl.BlockSpec((bsz, seq, n_e), lambda i: (0, 0, 0))],
        out_shape=[jax.ShapeDtypeStruct((rows, seq), I32),
                   jax.ShapeDtypeStruct((bsz, seq, n_e), I32)],
        compiler_params=_params(("arbitrary",)),
        name="route",
    )(probs_t.reshape(rows, seq))
    return slot_t.reshape(bsz, n_e, seq), slot


def _gather_kernel(slott_ref, xn_ref, xg_ref):
    n_e, cap = xg_ref.shape[0], xg_ref.shape[1]
    slot_t = slott_ref[...].astype(F32)
    slots = lax.broadcasted_iota(I32, (cap, 1), 0).astype(F32)
    xn = xn_ref[...]
    for e in range(n_e):
        onehot_t = (slot_t[e:e + 1, :] == slots).astype(BF16)
        xg_ref[e] = _dot(onehot_t, xn).astype(BF16)


def _gather(slot_t, xn, cap):
    bsz, n_e, seq = slot_t.shape
    return pl.pallas_call(
        _gather_kernel,
        grid=(bsz,),
        in_specs=[pl.BlockSpec((None, n_e, seq), lambda b: (b, 0, 0)),
                  pl.BlockSpec((None, seq, D_MODEL), lambda b: (b, 0, 0))],
        out_specs=pl.BlockSpec((n_e, None, cap, D_MODEL), lambda b: (0, b, 0, 0)),
        out_shape=jax.ShapeDtypeStruct((n_e, bsz, cap, D_MODEL), BF16),
        compiler_params=_params(("arbitrary",)),
        name="gather",
    )(slot_t, xn)


def _ffn_kernel(xg_ref, wg_ref, wu_ref, wd_ref, y_ref, acc_ref, wgb, wub, wdb, *, tr):
    f = pl.program_id(1)
    n_blocks = xg_ref.shape[0] // tr

    @pl.when(f == 0)
    def _():
        acc_ref[...] = jnp.zeros(acc_ref.shape, F32)

    def gate_up(r):
        x = xg_ref[r * tr:(r + 1) * tr, :]
        return _dot(x, wgb[...]), _dot(x, wub[...])

    wgb[...] = wg_ref[...].astype(BF16)
    gate0 = _dot(xg_ref[0:tr, :], wgb[...])
    wub[...] = wu_ref[...].astype(BF16)
    nxt = (gate0, _dot(xg_ref[0:tr, :], wub[...]))
    wdb[...] = wd_ref[...].astype(BF16)
    for r in range(n_blocks):
        gate, up = nxt
        if r + 1 < n_blocks:
            nxt = gate_up(r + 1)
        hidden = (gate * _sigmoid(gate) * up).astype(BF16)
        acc_ref[r * tr:(r + 1) * tr, :] += _dot(hidden, wdb[...])

    @pl.when(f == pl.num_programs(1) - 1)
    def _():
        y_ref[...] = acc_ref[...].astype(BF16)


def _ffn(xg, wg, wu, wd, tf=512, tr=512):
    n_e, rows, _ = xg.shape
    ff = wg.shape[2]
    return pl.pallas_call(
        functools.partial(_ffn_kernel, tr=tr),
        grid=(n_e, ff // tf),
        in_specs=[pl.BlockSpec((None, rows, D_MODEL), lambda e, f: (e, 0, 0)),
                  pl.BlockSpec((None, D_MODEL, tf), lambda e, f: (e, 0, f)),
                  pl.BlockSpec((None, D_MODEL, tf), lambda e, f: (e, 0, f)),
                  pl.BlockSpec((None, tf, D_MODEL), lambda e, f: (e, f, 0))],
        out_specs=pl.BlockSpec((None, rows, D_MODEL), lambda e, f: (e, 0, 0)),
        out_shape=jax.ShapeDtypeStruct((n_e, rows, D_MODEL), BF16),
        scratch_shapes=[pltpu.VMEM((rows, D_MODEL), F32),
                        pltpu.VMEM((D_MODEL, tf), BF16),
                        pltpu.VMEM((D_MODEL, tf), BF16),
                        pltpu.VMEM((tf, D_MODEL), BF16)],
        compiler_params=_params(("arbitrary", "arbitrary")),
        name="ffn",
    )(xg, wg, wu, wd)


def _combine_kernel(slot_ref, p_ref, y_ref, h_ref, nw_ref, out_ref):
    n_e, cap = y_ref.shape[0], y_ref.shape[1]
    slot = slot_ref[...].astype(F32)
    gates = p_ref[...]
    slots = lax.broadcasted_iota(I32, (1, cap), 1).astype(F32)
    acc = h_ref[...]
    for e in range(n_e):
        onehot = (slot[:, e:e + 1] == slots).astype(BF16)
        acc = acc + _dot(onehot, y_ref[e]) * gates[:, e:e + 1]
    out_ref[...] = _rms(acc, nw_ref[...])


def _combine(slot, probs, y, h, nw, tt=512):
    bsz, seq, n_e = slot.shape
    cap = y.shape[2]
    tok = lambda n: pl.BlockSpec((None, tt, n), lambda b, t: (b, t, 0))
    return pl.pallas_call(
        _combine_kernel,
        grid=(bsz, seq // tt),
        in_specs=[tok(n_e), tok(n_e),
                  pl.BlockSpec((n_e, None, cap, D_MODEL), lambda b, t: (0, b, 0, 0)),
                  tok(D_MODEL),
                  pl.BlockSpec(nw.shape, lambda b, t: (0, 0))],
        out_specs=tok(D_MODEL),
        out_shape=jax.ShapeDtypeStruct((bsz, seq, D_MODEL), F32),
        compiler_params=_params(("arbitrary", "arbitrary")),
        name="combine",
    )(slot, probs, y, h, nw)


def _layer(x, norm_mix_w, w_in, conv_w, a_log_fwd, dt_bias_fwd, a_log_bwd, dt_bias_bwd,
           head_norm_w, pool_w, pool_scale, w_out, norm_ffn_w, router_w,
           expert_w_gate, expert_w_up, expert_w_down, norm_final_w):
    bsz, seq, _ = x.shape
    nch = seq // CHUNK
    cap = EC_CAPACITY * seq // N_EXPERTS
    c_z = 3 * DELTA_WIDTH
    c_ab = c_z + DELTA_WIDTH
    c_u = c_ab + N_AB
    w_bf = w_in.astype(BF16)
    q, k, v, z, u, abt = _inproj(
        x, norm_mix_w.reshape(1, D_MODEL),
        w_bf[:, :c_z], w_bf[:, c_z:c_ab], w_bf[:, c_u:], w_bf[:, c_ab:c_u].T, conv_w)

    gate_params = jnp.stack([a_log_fwd, dt_bias_fwd, a_log_bwd, dt_bias_bwd]).astype(F32)
    o_delta = _delta(gate_params, q, k, v, z,
                     abt.reshape(bsz, N_AB, nch, CHUNK), head_norm_w.reshape(1, HEAD_DIM))

    h, xn, probs, probs_t = _mixout(u, o_delta, x,
                                    pool_w.astype(BF16), pool_scale.reshape(1, POOL_WIDTH),
                                    w_out.astype(BF16), norm_ffn_w.reshape(1, D_MODEL),
                                    router_w.astype(BF16))
    slot_t, slot = _route(probs_t, cap)
    xg = _gather(slot_t, xn, cap)
    y = _ffn(xg.reshape(N_EXPERTS, bsz * cap, D_MODEL), expert_w_gate, expert_w_up, expert_w_down)
    return _combine(slot, probs, y.reshape(N_EXPERTS, bsz, cap, D_MODEL), h,
                    norm_final_w.reshape(1, D_MODEL))


def kernel(x, norm_mix_w, w_in, conv_w, a_log_fwd, dt_bias_fwd, a_log_bwd, dt_bias_bwd,
           head_norm_w, pool_w, pool_scale, w_out, norm_ffn_w, router_w,
           expert_w_gate, expert_w_up, expert_w_down, norm_final_w):
    assert w_in.shape[0] == 1, "single-layer stack: the final norm is fused into the layer's last kernel"
    first = lambda a: a.reshape(a.shape[1:])
    return _layer(x, first(norm_mix_w), first(w_in), first(conv_w), first(a_log_fwd), first(dt_bias_fwd),
                  first(a_log_bwd), first(dt_bias_bwd), first(head_norm_w), first(pool_w),
                  first(pool_scale), first(w_out), first(norm_ffn_w), first(router_w),
                  first(expert_w_gate), first(expert_w_up), first(expert_w_down), norm_final_w)
```

```python
import functools

import jax
import jax.numpy as jnp
from jax import lax
from jax.experimental import pallas as pl
from jax.experimental.pallas import tpu as pltpu

F32 = jnp.float32
BF16 = jnp.bfloat16
I32 = jnp.int32

D_MODEL = 1024
N_HEADS = 4
HEAD_DIM = 128
DELTA_WIDTH = N_HEADS * HEAD_DIM
POOL_WINDOWS = (2, 4, 8, 16)
POOL_GROUP_DIM = 128
POOL_WIDTH = len(POOL_WINDOWS) * POOL_GROUP_DIM
SHORT_CONV = 5
N_EXPERTS = 16
EC_CAPACITY = 2
EXPERT_FF = 2 * D_MODEL
RMS_EPS = 1e-6

MXU_COLS = 256
SUBLANES = 8
CHUNK = 128
N_BISECT_GEO = 34
N_BISECT_LIN = 6
F32_TINY = 2.0 ** -126
BF16_ROWS = 16
COMBINE_TILE = 256
SLOT_WINDOW = 64
PACK = 4
HALO = 16
POOL_PAD = 16
N_AB = 4 * N_HEADS
N_COLQ = 4
N_COLQ_PAD = 8
VMEM_LIMIT = 56 * 1024 * 1024


def _sigmoid(x):
    return 1.0 / (1.0 + jnp.exp(-x))


def _softplus(x):
    return jnp.maximum(x, 0.0) + jnp.log(1.0 + jnp.exp(-jnp.abs(x)))


def _dot(a, b):
    return jnp.dot(a, b, preferred_element_type=F32)


def _dot_nt(a, b):
    return lax.dot_general(a, b, (((1,), (1,)), ((), ())), preferred_element_type=F32)


def _dot_tn(a, b):
    return lax.dot_general(a, b, (((0,), (0,)), ((), ())), preferred_element_type=F32)


def _rms(x, w):
    return x * lax.rsqrt(jnp.mean(x * x, axis=-1, keepdims=True) + RMS_EPS) * w


def _params(sem):
    return pltpu.CompilerParams(dimension_semantics=sem, vmem_limit_bytes=VMEM_LIMIT)


def _inproj_kernel(x_ref, xprev_ref, xnext_ref, nw_ref, wqkv_ref, wz_ref, wu_ref, wabt_ref, cw_ref,
                   q_ref, k_ref, v_ref, z_ref, u_ref, abt_ref):
    tm = x_ref.shape[0]
    i = pl.program_id(1)
    nw = nw_ref[...]
    x_ext = jnp.concatenate([xprev_ref[...], x_ref[...], xnext_ref[...]], axis=0)
    xn_ext = _rms(x_ext, nw).astype(BF16)
    xn = xn_ext[HALO:HALO + tm, :]
    n_ext = tm + 2 * HALO
    keep_before = jnp.where(i > 0, 1.0, 0.0)
    keep_after = jnp.where(i < pl.num_programs(1) - 1, 1.0, 0.0)

    def project(pair):
        cols = slice(pair * MXU_COLS, (pair + 1) * MXU_COLS)
        res = _dot(xn_ext, wqkv_ref[:, cols])
        return jnp.concatenate([res[0:HALO] * keep_before, res[HALO:HALO + tm],
                                res[HALO + tm:n_ext] * keep_after], axis=0)

    def conv_act(blk, ext):
        cols = slice(blk * HEAD_DIM, (blk + 1) * HEAD_DIM)
        acc = None
        for j in range(SHORT_CONV):
            shift = (SHORT_CONV // 2 - j) % n_ext
            rolled = ext if shift == 0 else pltpu.roll(ext, shift, 0)
            term = rolled[HALO:HALO + tm] * cw_ref[j:j + 1, cols]
            acc = term if acc is None else acc + term
        act = acc * _sigmoid(acc)
        kind, head = divmod(blk, N_HEADS)
        if kind == 0:
            scale = lax.rsqrt(jnp.sum(act * act, axis=-1, keepdims=True) + RMS_EPS) * (HEAD_DIM ** -0.5)
            q_ref[head] = (act * scale).astype(BF16)
        elif kind == 1:
            scale = lax.rsqrt(jnp.sum(act * act, axis=-1, keepdims=True) + RMS_EPS)
            k_ref[head] = (act * scale).astype(BF16)
        else:
            v_ref[head] = act.astype(BF16)

    n_pairs = 3 * DELTA_WIDTH // MXU_COLS
    heads_per_pair = MXU_COLS // HEAD_DIM
    nxt = project(0)
    for pair in range(n_pairs):
        cur = nxt
        if pair + 1 < n_pairs:
            nxt = project(pair + 1)
        else:
            z = _dot(xn, wz_ref[...])
            for head in range(N_HEADS):
                z_ref[head] = z[:, head * HEAD_DIM:(head + 1) * HEAD_DIM].astype(BF16)
            u_ref[...] = _dot(xn, wu_ref[...]).astype(BF16)
            abt_ref[...] = _dot_nt(wabt_ref[...], xn)
        for sub in range(heads_per_pair):
            conv_act(pair * heads_per_pair + sub, cur[:, sub * HEAD_DIM:(sub + 1) * HEAD_DIM])


def _inproj(x, nw, wqkv, wz, wu, wabt, conv_w, tm=512):
    bsz, seq, _ = x.shape
    full = lambda a: pl.BlockSpec(a.shape, lambda b, i: (0, 0))
    heads = pl.BlockSpec((None, N_HEADS, tm, HEAD_DIM), lambda b, i: (b, 0, i, 0))
    halo_per_tile = tm // HALO
    head_shape = jax.ShapeDtypeStruct((bsz, N_HEADS, seq, HEAD_DIM), BF16)
    return pl.pallas_call(
        _inproj_kernel,
        grid=(bsz, seq // tm),
        in_specs=[pl.BlockSpec((None, tm, D_MODEL), lambda b, i: (b, i, 0)),
                  pl.BlockSpec((None, HALO, D_MODEL),
                               lambda b, i: (b, jnp.maximum(i * halo_per_tile - 1, 0), 0)),
                  pl.BlockSpec((None, HALO, D_MODEL),
                               lambda b, i: (b, jnp.minimum((i + 1) * halo_per_tile, seq // HALO - 1), 0)),
                  full(nw), full(wqkv), full(wz), full(wu), full(wabt), full(conv_w)],
        out_specs=[heads, heads, heads, heads,
                   pl.BlockSpec((None, tm, POOL_WIDTH), lambda b, i: (b, i, 0)),
                   pl.BlockSpec((None, N_AB, tm), lambda b, i: (b, 0, i))],
        out_shape=[head_shape, head_shape, head_shape, head_shape,
                   jax.ShapeDtypeStruct((bsz, seq, POOL_WIDTH), BF16),
                   jax.ShapeDtypeStruct((bsz, N_AB, seq), F32)],
        compiler_params=_params(("arbitrary", "arbitrary")),
        name="inproj",
    )(x, x, x, nw, wqkv, wz, wu, wabt, conv_w)


def _block_rows(t, size, parity):
    n = t.shape[0]
    return jnp.concatenate([t[r:r + size] for r in range(parity * size, n, 2 * size)], axis=0)


def _merge_block_rows(t, new_rows, size, parity):
    n = t.shape[0]
    parts = []
    for i, r in enumerate(range(0, n, 2 * size)):
        new = new_rows[i * size:(i + 1) * size]
        old = t[r + (1 - parity) * size:r + (2 - parity) * size]
        parts += [old, new] if parity else [new, old]
    return jnp.concatenate(parts, axis=0)


def _delta_kernel(gp_ref, q_ref, k_ref, v_ref, z_ref, abt_ref, hnw_ref, out_ref,
                  rowb, lvl, sm_s, sn_s, oq_s, o_s):
    n_heads, seq = q_ref.shape[0], q_ref.shape[1]
    nch = seq // CHUNK
    n_lvl = lvl.shape[0]
    ri = lax.broadcasted_iota(I32, (CHUNK, CHUNK), 0)
    ci = lax.broadcasted_iota(I32, (CHUNK, CHUNK), 1)

    def gate_rows(h):
        for d in range(2):
            a_row = abt_ref[2 * d * N_HEADS + h]
            b_row = abt_ref[(2 * d + 1) * N_HEADS + h]
            a_log = jnp.full((1, CHUNK), gp_ref[2 * d, h], F32)
            g_row = -jnp.exp(a_log) * _softplus(a_row + gp_ref[2 * d + 1, h])
            tri = (ri <= ci) if d == 0 else (ri >= ci)
            gc_row = jnp.dot(g_row, tri.astype(F32), precision=lax.Precision.HIGHEST,
                             preferred_element_type=F32)
            g_last = gc_row[:, CHUNK - 1:CHUNK] if d == 0 else gc_row[:, 0:1]
            g_last = jnp.broadcast_to(g_last, (nch, CHUNK))
            rowb[d, 0] = gc_row
            rowb[d, 1] = g_last
            rowb[d, 2] = jnp.exp(g_last)
            rowb[d, 3] = _sigmoid(b_row)

    sel_r = lax.broadcasted_iota(I32, (2 * N_COLQ_PAD, N_COLQ * CHUNK), 0)
    sel_c = lax.broadcasted_iota(I32, (2 * N_COLQ_PAD, N_COLQ * CHUNK), 1)
    col_sel = ((sel_r & (N_COLQ_PAD - 1)) == (sel_c >> (CHUNK.bit_length() - 1))).astype(BF16)
    pad_rows = jnp.zeros((N_COLQ_PAD - N_COLQ, CHUNK), F32)

    for l in range(n_lvl):
        same_pair = (ri >> (l + 1)) == (ci >> (l + 1))
        other_half = (ri >> l) != (ci >> l)
        lvl[l] = (same_pair & other_half).astype(F32)
    eye = (ri == ci).astype(F32)

    def prep_head(h, carry):
        gate_rows(h)
        chains = []
        for c in range(nch):
            r0 = c * CHUNK
            qb = q_ref[h, r0:r0 + CHUNK, :]
            kb = k_ref[h, r0:r0 + CHUNK, :]
            q = qb.astype(F32)
            k = kb.astype(F32)
            v = v_ref[h, r0:r0 + CHUNK, :].astype(F32)
            kk = _dot_nt(kb, kb)
            qk = _dot_nt(qb, kb)
            rows = jnp.concatenate(
                [rowb[0, 0, pl.ds(c, 1), :], rowb[0, 3, pl.ds(c, 1), :],
                 rowb[1, 0, pl.ds(c, 1), :], rowb[1, 3, pl.ds(c, 1), :], pad_rows], axis=0)
            rows_hi = rows.astype(BF16).astype(F32)
            split = jnp.concatenate([rows_hi, rows - rows_hi], axis=0).astype(BF16)
            cols = _dot_tn(split, col_sel)
            for d in range(2):
                incl = (ri >= ci) if d == 0 else (ri <= ci)
                strict = (ri > ci) if d == 0 else (ri < ci)
                gcol = cols[:, 2 * d * CHUNK:(2 * d + 1) * CHUNK]
                grow = rowb[d, 0, pl.ds(c, 1), :]
                beta = cols[:, (2 * d + 1) * CHUNK:(2 * d + 2) * CHUNK]
                decay = jnp.where(incl, jnp.exp(jnp.where(incl, gcol - grow, 0.0)), 0.0)
                a_mat = jnp.where(strict, beta * kk * decay, 0.0)
                egc = jnp.exp(gcol)
                g_last = rowb[d, 1, pl.ds(c, 1), :]
                chains.append(dict(
                    c=c, d=d, r0=r0, a=a_mat,
                    rhs=jnp.concatenate([k * (beta * egc), v * beta], axis=1).astype(BF16),
                    qkm=jnp.where(incl, qk * decay, 0.0).astype(BF16),
                    kd=(k * jnp.exp(g_last - gcol)).astype(BF16),
                    qd=q * egc))
        t_inv = [eye - ch["a"] * lvl[0] for ch in chains]
        for l in range(1, n_lvl):
            size = 1 << l
            t_b = [t.astype(BF16) for t in t_inv]
            c_b = [(ch["a"] * lvl[l]).astype(BF16) for ch in chains]
            if size < SUBLANES:
                xs = [_dot(tb, cb) for tb, cb in zip(t_b, c_b)]
                ys = [_dot(x.astype(BF16), tb) for x, tb in zip(xs, t_b)]
                t_inv = [t - y for t, y in zip(t_inv, ys)]
            else:
                halves = [_block_rows(t, size, 1 - ch["d"]) for t, ch in zip(t_inv, chains)]
                xs = [_dot(hf.astype(BF16), cb) for hf, cb in zip(halves, c_b)]
                ys = [_dot(x.astype(BF16), tb) for x, tb in zip(xs, t_b)]
                t_inv = [_merge_block_rows(t, hf - y, size, 1 - ch["d"])
                         for t, hf, y, ch in zip(t_inv, halves, ys, chains)]
        wus = [_dot(t.astype(BF16), ch["rhs"]).astype(BF16) for t, ch in zip(t_inv, chains)]
        kwus = [_dot_tn(ch["kd"], wu) for ch, wu in zip(chains, wus)]
        qwus = [_dot(ch["qkm"], wu) for ch, wu in zip(chains, wus)]
        for ch, kwu, qwu in zip(chains, kwus, qwus):
            c, d, r0 = ch["c"], ch["d"], ch["r0"]
            gamma = rowb[d, 2, pl.ds(c, 1), :]
            sm_s[h, d, c] = (eye * gamma - kwu[:, :HEAD_DIM]).astype(BF16)
            sn_s[h, d, c] = kwu[:, HEAD_DIM:].astype(BF16)
            oq_s[h, d, r0:r0 + CHUNK, :] = (ch["qd"] - qwu[:, :HEAD_DIM]).astype(BF16)
        for c in range(nch):
            o_s[h, c * CHUNK:(c + 1) * CHUNK, :] = qwus[2 * c][:, HEAD_DIM:] + qwus[2 * c + 1][:, HEAD_DIM:]
        return carry

    lax.fori_loop(0, n_heads, prep_head, 0)

    def scan_step(i, states):
        new_states = []
        for h in range(n_heads):
            for d in range(2):
                c = i if d == 0 else nch - 1 - i
                r0 = pl.multiple_of(c * CHUNK, CHUNK)
                s_b = states[2 * h + d].astype(BF16)
                o_s[h, pl.ds(r0, CHUNK), :] += _dot(oq_s[h, d, pl.ds(r0, CHUNK), :], s_b)
                new_states.append(sn_s[h, d, c].astype(F32) + _dot(sm_s[h, d, c], s_b))
        return tuple(new_states)

    zero_state = jnp.zeros((HEAD_DIM, HEAD_DIM), F32)
    lax.fori_loop(0, nch, scan_step, (zero_state,) * (2 * n_heads), unroll=2)

    for h in range(n_heads):
        o = o_s[h]
        zg = z_ref[h].astype(F32)
        out_ref[:, h * HEAD_DIM:(h + 1) * HEAD_DIM] = (
            _rms(o, hnw_ref[...]) * (zg * _sigmoid(zg))).astype(out_ref.dtype)


def _delta(gate_params, q, k, v, z, abt, hnw):
    bsz, n_heads, seq, _ = q.shape
    nch = seq // CHUNK
    n_lvl = CHUNK.bit_length() - 1
    heads = pl.BlockSpec((None, n_heads, seq, HEAD_DIM), lambda b: (b, 0, 0, 0))
    return pl.pallas_call(
        _delta_kernel,
        grid=(bsz,),
        in_specs=[pl.BlockSpec(memory_space=pltpu.SMEM),
                  heads, heads, heads, heads,
                  pl.BlockSpec((None, N_AB, nch, CHUNK), lambda b: (b, 0, 0, 0)),
                  pl.BlockSpec(hnw.shape, lambda b: (0, 0))],
        out_specs=pl.BlockSpec((None, seq, n_heads * HEAD_DIM), lambda b: (b, 0, 0)),
        out_shape=jax.ShapeDtypeStruct((bsz, seq, n_heads * HEAD_DIM), BF16),
        scratch_shapes=[
            pltpu.VMEM((2, N_COLQ, nch, CHUNK), F32),
            pltpu.VMEM((n_lvl, CHUNK, CHUNK), F32),
            pltpu.VMEM((n_heads, 2, nch, HEAD_DIM, HEAD_DIM), BF16),
            pltpu.VMEM((n_heads, 2, nch, HEAD_DIM, HEAD_DIM), BF16),
            pltpu.VMEM((n_heads, 2, seq, HEAD_DIM), BF16),
            pltpu.VMEM((n_heads, seq, HEAD_DIM), F32),
        ],
        compiler_params=_params(("arbitrary",)),
        name="delta",
    )(gate_params, q, k, v, z, abt, hnw)


def _mixout_kernel(u_ref, od_ref, x_ref, pw_ref, ps_ref, wout_ref, nw_ref, rwt_ref,
                   h_ref, xn_ref, pt_ref, upad):
    seq = u_ref.shape[0]
    tm = od_ref.shape[0]
    i = pl.program_id(1)
    n_tiles = pl.num_programs(1)
    t0 = pl.multiple_of(i * tm, tm)

    upad[POOL_PAD:POOL_PAD + tm, :] = u_ref[pl.ds(t0, tm), :].astype(F32)
    before = u_ref[pl.ds(pl.multiple_of(jnp.maximum(t0 - POOL_PAD, 0), POOL_PAD), POOL_PAD), :].astype(F32)
    after = u_ref[pl.ds(pl.multiple_of(jnp.minimum(t0 + tm, seq - POOL_PAD), POOL_PAD), POOL_PAD), :].astype(F32)
    upad[0:POOL_PAD, :] = jnp.where(i > 0, before, 0.0)
    upad[POOL_PAD + tm:POOL_PAD + tm + POOL_PAD, :] = jnp.where(i < n_tiles - 1, after, 0.0)

    tglob = t0 + lax.broadcasted_iota(I32, (tm, 1), 0)
    pooled_out = []
    for g, window in enumerate(POOL_WINDOWS):
        lo = window // 2
        hi = window - lo - 1
        cols = slice(g * POOL_GROUP_DIM, (g + 1) * POOL_GROUP_DIM)
        total = None
        for dlt in range(-lo, hi + 1):
            term = upad[POOL_PAD + dlt:POOL_PAD + dlt + tm, cols]
            total = term if total is None else total + term
        count = (jnp.minimum(tglob + hi + 1, seq) - jnp.maximum(tglob - lo, 0)).astype(F32)
        diff = total / count - upad[POOL_PAD:POOL_PAD + tm, cols]
        pooled_out.append(_dot(diff.astype(BF16), pw_ref[g]))
    o_pool = jnp.concatenate(pooled_out, axis=1) * ps_ref[...]

    h = (x_ref[...]
         + _dot(od_ref[...], wout_ref[0:DELTA_WIDTH, :])
         + _dot(o_pool.astype(BF16), wout_ref[DELTA_WIDTH:DELTA_WIDTH + POOL_WIDTH, :]))
    h_ref[...] = h
    xn = _rms(h, nw_ref[...]).astype(BF16)
    xn_ref[...] = xn
    logits_t = _dot_nt(rwt_ref[...], xn)
    e_t = jnp.exp(logits_t - jnp.max(logits_t, axis=0, keepdims=True))
    pt_ref[...] = e_t / jnp.sum(e_t, axis=0, keepdims=True)


def _mixout(u, od, x, pw, ps, wout, nw, rw, tm=512):
    bsz, seq, _ = x.shape
    rwt = rw.T
    full = lambda a: pl.BlockSpec(a.shape, lambda b, i: (0,) * a.ndim)
    tile = lambda n: pl.BlockSpec((None, tm, n), lambda b, i: (b, i, 0))
    return pl.pallas_call(
        _mixout_kernel,
        grid=(bsz, seq // tm),
        in_specs=[pl.BlockSpec((None, seq, POOL_WIDTH), lambda b, i: (b, 0, 0)),
                  tile(DELTA_WIDTH), tile(D_MODEL), full(pw), full(ps), full(wout), full(nw), full(rwt)],
        out_specs=[tile(D_MODEL), tile(D_MODEL),
                   pl.BlockSpec((None, N_EXPERTS, tm), lambda b, i: (b, 0, i))],
        out_shape=[jax.ShapeDtypeStruct((bsz, seq, D_MODEL), F32),
                   jax.ShapeDtypeStruct((bsz, seq, D_MODEL), BF16),
                   jax.ShapeDtypeStruct((bsz, N_EXPERTS, seq), F32)],
        scratch_shapes=[pltpu.VMEM((tm + 2 * POOL_PAD, POOL_WIDTH), F32)],
        compiler_params=_params(("arbitrary", "arbitrary")),
        name="mixout",
    )(u, od, x, pw, ps, wout, nw, rwt)


def _route_kernel(pt_ref, slott_ref, slot_ref, cnt_ref, *, cap):
    p = pt_ref[...]
    rows, seq = p.shape
    n_e = slot_ref.shape[2]

    def bracket(mid_fn):
        def step(_, bounds):
            lo, hi = bounds
            mid = mid_fn(lo, hi)
            enough = jnp.sum((p >= mid).astype(F32), axis=1, keepdims=True) >= cap
            return jnp.where(enough, mid, lo), jnp.where(enough, hi, mid)
        return step

    bounds = (jnp.zeros((rows, 1), F32), jnp.full((rows, 1), 2.0, F32))
    bounds = lax.fori_loop(0, N_BISECT_GEO, bracket(lambda lo, hi: jnp.sqrt(jnp.maximum(lo, F32_TINY) * hi)),
                           bounds)
    lo, hi = lax.fori_loop(0, N_BISECT_LIN, bracket(lambda lo, hi: 0.5 * (lo + hi)), bounds)

    above = p >= hi
    tied = (p >= lo) & (p < hi)
    need = cap - jnp.sum(above.astype(F32), axis=1, keepdims=True)
    ri = lax.broadcasted_iota(I32, (CHUNK, CHUNK), 0)
    ci = lax.broadcasted_iota(I32, (CHUNK, CHUNK), 1)
    tri_b = (ri <= ci).astype(BF16)

    def prefix_count(mask):
        carry = jnp.zeros((rows, 1), F32)
        blocks = []
        for c0 in range(0, seq, CHUNK):
            part = _dot(mask[:, c0:c0 + CHUNK].astype(BF16), tri_b) + carry
            blocks.append(part)
            carry = part[:, CHUNK - 1:CHUNK]
        return jnp.concatenate(blocks, axis=1)

    tied_f = tied.astype(F32)
    sel = above | (tied & (prefix_count(tied_f) - tied_f < need))
    pos = prefix_count(sel.astype(F32))
    slot_t = jnp.where(sel, pos - 1.0, -1.0)
    slott_ref[...] = slot_t.astype(I32)
    tt = seq // cnt_ref.shape[1]
    cnt_ref[...] = jnp.concatenate([pos[:, t1 - 1:t1] for t1 in range(tt, seq + 1, tt)], axis=1).astype(I32)
    eye_b = (lax.broadcasted_iota(I32, (n_e, n_e), 0) == lax.broadcasted_iota(I32, (n_e, n_e), 1)).astype(BF16)
    for b in range(rows // n_e):
        slot_ref[b] = _dot_tn(slot_t[b * n_e:(b + 1) * n_e, :].astype(BF16), eye_b).astype(I32)


def _route(probs_t, cap):
    bsz, n_e, seq = probs_t.shape
    rows = bsz * n_e
    n_tiles = seq // COMBINE_TILE
    slot_t, slot, tile_cnt = pl.pallas_call(
        functools.partial(_route_kernel, cap=cap),
        grid=(1,),
        in_specs=[pl.BlockSpec((rows, seq), lambda i: (0, 0))],
        out_specs=[pl.BlockSpec((rows, seq), lambda i: (0, 0)),
                   pl.BlockSpec((bsz, seq, n_e), lambda i: (0, 0, 0)),
                   pl.BlockSpec((rows, n_tiles), lambda i: (0, 0))],
        out_shape=[jax.ShapeDtypeStruct((rows, seq), I32),
                   jax.ShapeDtypeStruct((bsz, seq, n_e), I32),
                   jax.ShapeDtypeStruct((rows, n_tiles), I32)],
        compiler_params=_params(("arbitrary",)),
        name="route",
    )(probs_t.reshape(rows, seq))
    return slot_t.reshape(bsz, n_e, seq), slot, tile_cnt


def _gather_kernel(slott_ref, pt_ref, xn_ref, xg_ref, gate_ref):
    n_e, cap = xg_ref.shape[0], xg_ref.shape[1]
    slot_t = slott_ref[...].astype(F32)
    probs_t = pt_ref[...]
    slots = lax.broadcasted_iota(I32, (cap, 1), 0).astype(F32)
    xn = xn_ref[...]
    for e in range(n_e):
        hit = slot_t[e:e + 1, :] == slots
        xg_ref[e] = _dot(hit.astype(BF16), xn).astype(BF16)
        gate_ref[e] = jnp.sum(jnp.where(hit, probs_t[e:e + 1, :], 0.0), axis=1, keepdims=True)


def _gather(slot_t, probs_t, xn, cap):
    bsz, n_e, seq = slot_t.shape
    rows = pl.BlockSpec((None, n_e, seq), lambda b: (b, 0, 0))
    return pl.pallas_call(
        _gather_kernel,
        grid=(bsz,),
        in_specs=[rows, rows, pl.BlockSpec((None, seq, D_MODEL), lambda b: (b, 0, 0))],
        out_specs=[pl.BlockSpec((n_e, None, cap, D_MODEL), lambda b: (0, b, 0, 0)),
                   pl.BlockSpec((n_e, None, cap, 1), lambda b: (0, b, 0, 0))],
        out_shape=[jax.ShapeDtypeStruct((n_e, bsz, cap, D_MODEL), BF16),
                   jax.ShapeDtypeStruct((n_e, bsz, cap, 1), F32)],
        compiler_params=_params(("arbitrary",)),
        name="gather",
    )(slot_t, probs_t, xn)


def _ffn_kernel(xg_ref, gate_ref, wg_ref, wu_ref, wd_ref, y_ref, acc_ref, wgb, wub, wdb, *, tr):
    f = pl.program_id(1)
    n_blocks = xg_ref.shape[0] // tr

    @pl.when(f == 0)
    def _():
        acc_ref[...] = jnp.zeros(acc_ref.shape, F32)

    def gate_up(r):
        x = xg_ref[r * tr:(r + 1) * tr, :]
        return _dot(x, wgb[...]), _dot(x, wub[...])

    wgb[...] = wg_ref[...].astype(BF16)
    gate0 = _dot(xg_ref[0:tr, :], wgb[...])
    wub[...] = wu_ref[...].astype(BF16)
    nxt = (gate0, _dot(xg_ref[0:tr, :], wub[...]))
    wdb[...] = wd_ref[...].astype(BF16)
    for r in range(n_blocks):
        gate, up = nxt
        if r + 1 < n_blocks:
            nxt = gate_up(r + 1)
        hidden = (gate * _sigmoid(gate) * up).astype(BF16)
        acc_ref[r * tr:(r + 1) * tr, :] += _dot(hidden, wdb[...])

    @pl.when(f == pl.num_programs(1) - 1)
    def _():
        y_ref[...] = (acc_ref[...] * gate_ref[...]).astype(BF16)


def _ffn(xg, gates, wg, wu, wd, tf=512, tr=512):
    n_e, rows, _ = xg.shape
    ff = wg.shape[2]
    return pl.pallas_call(
        functools.partial(_ffn_kernel, tr=tr),
        grid=(n_e, ff // tf),
        in_specs=[pl.BlockSpec((None, rows, D_MODEL), lambda e, f: (e, 0, 0)),
                  pl.BlockSpec((None, rows, 1), lambda e, f: (e, 0, 0)),
                  pl.BlockSpec((None, D_MODEL, tf), lambda e, f: (e, 0, f)),
                  pl.BlockSpec((None, D_MODEL, tf), lambda e, f: (e, 0, f)),
                  pl.BlockSpec((None, tf, D_MODEL), lambda e, f: (e, f, 0))],
        out_specs=pl.BlockSpec((None, rows, D_MODEL), lambda e, f: (e, 0, 0)),
        out_shape=jax.ShapeDtypeStruct((n_e, rows, D_MODEL), BF16),
        scratch_shapes=[pltpu.VMEM((rows, D_MODEL), F32),
                        pltpu.VMEM((D_MODEL, tf), BF16),
                        pltpu.VMEM((D_MODEL, tf), BF16),
                        pltpu.VMEM((tf, D_MODEL), BF16)],
        compiler_params=_params(("arbitrary", "arbitrary")),
        name="ffn",
    )(xg, gates, wg, wu, wd)


def _combine_kernel(cnt_ref, slot_ref, y_ref, h_ref, nw_ref, out_ref):
    n_e, cap = y_ref.shape[0], y_ref.shape[1]
    b, tile = pl.program_id(0), pl.program_id(1)
    slot = slot_ref[...].astype(F32)
    packed = PACK * SLOT_WINDOW
    lane = lax.broadcasted_iota(I32, (1, packed), 1)
    lane_f = lane.astype(F32)
    all_slots = lax.broadcasted_iota(I32, (1, cap), 1).astype(F32)
    acc = h_ref[...]
    overflows = []
    for group in range(n_e // PACK):
        target = jnp.full((1, packed), -1.0, F32)
        windows = []
        for j in range(PACK):
            e = group * PACK + j
            hi = cnt_ref[b * n_e + e, tile]
            lo = jnp.where(tile > 0, cnt_ref[b * n_e + e, jnp.maximum(tile - 1, 0)], 0)
            base = jnp.minimum(lo - (lo & (BF16_ROWS - 1)), cap - SLOT_WINDOW)
            overflow = hi - base > SLOT_WINDOW
            base_f = base.astype(F32)
            end_f = jnp.where(overflow, base_f, base_f + SLOT_WINDOW)
            col = slot[:, e:e + 1]
            inside = (col >= base_f) & (col < end_f)
            local = jnp.where(inside, col - base_f + j * SLOT_WINDOW, -1.0)
            target = jnp.where((lane >> (SLOT_WINDOW.bit_length() - 1)) == j, local, target)
            windows.append(y_ref[e, pl.ds(pl.multiple_of(base, BF16_ROWS), SLOT_WINDOW), :])
            overflows.append(overflow)
        onehot = (target == lane_f).astype(BF16)
        acc = acc + _dot(onehot, jnp.concatenate(windows, axis=0))
    out_ref[...] = acc
    for e in range(n_e):

        @pl.when(overflows[e])
        def _():
            full = (slot_ref[:, e:e + 1].astype(F32) == all_slots).astype(BF16)
            out_ref[...] += _dot(full, y_ref[e])

    out_ref[...] = _rms(out_ref[...], nw_ref[...])


def _combine(tile_cnt, slot, y, h, nw):
    bsz, seq, n_e = slot.shape
    cap = y.shape[2]
    tt = COMBINE_TILE
    tok = lambda n: pl.BlockSpec((None, tt, n), lambda b, t, cnt: (b, t, 0))
    return pl.pallas_call(
        _combine_kernel,
        grid_spec=pltpu.PrefetchScalarGridSpec(
            num_scalar_prefetch=1,
            grid=(bsz, seq // tt),
            in_specs=[tok(n_e),
                      pl.BlockSpec((n_e, None, cap, D_MODEL), lambda b, t, cnt: (0, b, 0, 0)),
                      tok(D_MODEL),
                      pl.BlockSpec(nw.shape, lambda b, t, cnt: (0, 0))],
            out_specs=tok(D_MODEL)),
        out_shape=jax.ShapeDtypeStruct((bsz, seq, D_MODEL), F32),
        compiler_params=_params(("arbitrary", "arbitrary")),
        name="combine",
    )(tile_cnt, slot, y, h, nw)


def _layer(x, norm_mix_w, w_in, conv_w, a_log_fwd, dt_bias_fwd, a_log_bwd, dt_bias_bwd,
           head_norm_w, pool_w, pool_scale, w_out, norm_ffn_w, router_w,
           expert_w_gate, expert_w_up, expert_w_down, norm_final_w):
    bsz, seq, _ = x.shape
    nch = seq // CHUNK
    cap = EC_CAPACITY * seq // N_EXPERTS
    c_z = 3 * DELTA_WIDTH
    c_ab = c_z + DELTA_WIDTH
    c_u = c_ab + N_AB
    w_bf = w_in.astype(BF16)
    q, k, v, z, u, abt = _inproj(
        x, norm_mix_w.reshape(1, D_MODEL),
        w_bf[:, :c_z], w_bf[:, c_z:c_ab], w_bf[:, c_u:], w_bf[:, c_ab:c_u].T, conv_w)

    gate_params = jnp.stack([a_log_fwd, dt_bias_fwd, a_log_bwd, dt_bias_bwd]).astype(F32)
    o_delta = _delta(gate_params, q, k, v, z,
                     abt.reshape(bsz, N_AB, nch, CHUNK), head_norm_w.reshape(1, HEAD_DIM))

    h, xn, probs_t = _mixout(u, o_delta, x,
                             pool_w.astype(BF16), pool_scale.reshape(1, POOL_WIDTH),
                             w_out.astype(BF16), norm_ffn_w.reshape(1, D_MODEL), router_w.astype(BF16))
    slot_t, slot, tile_cnt = _route(probs_t, cap)
    xg, gates = _gather(slot_t, probs_t, xn, cap)
    y = _ffn(xg.reshape(N_EXPERTS, bsz * cap, D_MODEL), gates.reshape(N_EXPERTS, bsz * cap, 1),
             expert_w_gate, expert_w_up, expert_w_down)
    return _combine(tile_cnt, slot, y.reshape(N_EXPERTS, bsz, cap, D_MODEL), h,
                    norm_final_w.reshape(1, D_MODEL))


def kernel(x, norm_mix_w, w_in, conv_w, a_log_fwd, dt_bias_fwd, a_log_bwd, dt_bias_bwd,
           head_norm_w, pool_w, pool_scale, w_out, norm_ffn_w, router_w,
           expert_w_gate, expert_w_up, expert_w_down, norm_final_w):
    assert w_in.shape[0] == 1, "single-layer stack: the final norm is fused into the layer's last kernel"
    first = lambda a: a.reshape(a.shape[1:])
    return _layer(x, first(norm_mix_w), first(w_in), first(conv_w), first(a_log_fwd), first(dt_bias_fwd),
                  first(a_log_bwd), first(dt_bias_bwd), first(head_norm_w), first(pool_w),
                  first(pool_scale), first(w_out), first(norm_ffn_w), first(router_w),
                  first(expert_w_gate), first(expert_w_up), first(expert_w_down), norm_final_w)
```

```python
import functools

import jax
import jax.numpy as jnp
from jax import lax
from jax.experimental import pallas as pl
from jax.experimental.pallas import tpu as pltpu

F32 = jnp.float32
BF16 = jnp.bfloat16
I32 = jnp.int32

D_MODEL = 1024
N_HEADS = 4
HEAD_DIM = 128
DELTA_WIDTH = N_HEADS * HEAD_DIM
POOL_WINDOWS = (2, 4, 8, 16)
POOL_GROUP_DIM = 128
POOL_WIDTH = len(POOL_WINDOWS) * POOL_GROUP_DIM
SHORT_CONV = 5
N_EXPERTS = 16
EC_CAPACITY = 2
EXPERT_FF = 2 * D_MODEL
RMS_EPS = 1e-6

MXU_COLS = 256
SUBLANES = 8
CHUNK = 128
N_BISECT_GEO = 34
N_BISECT_LIN = 6
F32_TINY = 2.0 ** -126
BF16_ROWS = 16
COMBINE_TILE = 256
SLOT_WINDOW = 64
PACK = 4
HALO = 16
POOL_PAD = 16
N_AB = 4 * N_HEADS
N_COLQ = 4
N_COLQ_PAD = 8
VMEM_LIMIT = 56 * 1024 * 1024


def _sigmoid(x):
    return 1.0 / (1.0 + jnp.exp(-x))


def _softplus(x):
    return jnp.maximum(x, 0.0) + jnp.log(1.0 + jnp.exp(-jnp.abs(x)))


def _dot(a, b):
    return jnp.dot(a, b, preferred_element_type=F32)


def _dot_nt(a, b):
    return lax.dot_general(a, b, (((1,), (1,)), ((), ())), preferred_element_type=F32)


def _dot_tn(a, b):
    return lax.dot_general(a, b, (((0,), (0,)), ((), ())), preferred_element_type=F32)


def _rms(x, w):
    return x * lax.rsqrt(jnp.mean(x * x, axis=-1, keepdims=True) + RMS_EPS) * w


def _params(sem):
    return pltpu.CompilerParams(dimension_semantics=sem, vmem_limit_bytes=VMEM_LIMIT)


def _inproj_kernel(x_ref, xprev_ref, xnext_ref, nw_ref, wqkv_ref, wz_ref, wu_ref, wabt_ref, cw_ref,
                   q_ref, k_ref, v_ref, z_ref, u_ref, abt_ref):
    tm = x_ref.shape[0]
    i = pl.program_id(1)
    nw = nw_ref[...]
    x_ext = jnp.concatenate([xprev_ref[...], x_ref[...], xnext_ref[...]], axis=0)
    xn_ext = _rms(x_ext, nw).astype(BF16)
    xn = xn_ext[HALO:HALO + tm, :]
    n_ext = tm + 2 * HALO
    keep_before = jnp.where(i > 0, 1.0, 0.0)
    keep_after = jnp.where(i < pl.num_programs(1) - 1, 1.0, 0.0)

    def project(pair):
        cols = slice(pair * MXU_COLS, (pair + 1) * MXU_COLS)
        res = _dot(xn_ext, wqkv_ref[:, cols])
        return jnp.concatenate([res[0:HALO] * keep_before, res[HALO:HALO + tm],
                                res[HALO + tm:n_ext] * keep_after], axis=0)

    def conv_act(blk, ext):
        cols = slice(blk * HEAD_DIM, (blk + 1) * HEAD_DIM)
        acc = None
        for j in range(SHORT_CONV):
            shift = (SHORT_CONV // 2 - j) % n_ext
            rolled = ext if shift == 0 else pltpu.roll(ext, shift, 0)
            term = rolled[HALO:HALO + tm] * cw_ref[j:j + 1, cols]
            acc = term if acc is None else acc + term
        act = acc * _sigmoid(acc)
        kind, head = divmod(blk, N_HEADS)
        if kind == 0:
            scale = lax.rsqrt(jnp.sum(act * act, axis=-1, keepdims=True) + RMS_EPS) * (HEAD_DIM ** -0.5)
            q_ref[head] = (act * scale).astype(BF16)
        elif kind == 1:
            scale = lax.rsqrt(jnp.sum(act * act, axis=-1, keepdims=True) + RMS_EPS)
            k_ref[head] = (act * scale).astype(BF16)
        else:
            v_ref[head] = act.astype(BF16)

    n_pairs = 3 * DELTA_WIDTH // MXU_COLS
    heads_per_pair = MXU_COLS // HEAD_DIM
    nxt = project(0)
    for pair in range(n_pairs):
        cur = nxt
        if pair + 1 < n_pairs:
            nxt = project(pair + 1)
        else:
            z = _dot(xn, wz_ref[...])
            for head in range(N_HEADS):
                z_ref[head] = z[:, head * HEAD_DIM:(head + 1) * HEAD_DIM].astype(BF16)
            u_ref[...] = _dot(xn, wu_ref[...]).astype(BF16)
            abt_ref[...] = _dot_nt(wabt_ref[...], xn)
        for sub in range(heads_per_pair):
            conv_act(pair * heads_per_pair + sub, cur[:, sub * HEAD_DIM:(sub + 1) * HEAD_DIM])


def _inproj(x, nw, wqkv, wz, wu, wabt, conv_w, tm=512):
    bsz, seq, _ = x.shape
    full = lambda a: pl.BlockSpec(a.shape, lambda b, i: (0, 0))
    heads = pl.BlockSpec((None, N_HEADS, tm, HEAD_DIM), lambda b, i: (b, 0, i, 0))
    halo_per_tile = tm // HALO
    head_shape = jax.ShapeDtypeStruct((bsz, N_HEADS, seq, HEAD_DIM), BF16)
    return pl.pallas_call(
        _inproj_kernel,
        grid=(bsz, seq // tm),
        in_specs=[pl.BlockSpec((None, tm, D_MODEL), lambda b, i: (b, i, 0)),
                  pl.BlockSpec((None, HALO, D_MODEL),
                               lambda b, i: (b, jnp.maximum(i * halo_per_tile - 1, 0), 0)),
                  pl.BlockSpec((None, HALO, D_MODEL),
                               lambda b, i: (b, jnp.minimum((i + 1) * halo_per_tile, seq // HALO - 1), 0)),
                  full(nw), full(wqkv), full(wz), full(wu), full(wabt), full(conv_w)],
        out_specs=[heads, heads, heads, heads,
                   pl.BlockSpec((None, tm, POOL_WIDTH), lambda b, i: (b, i, 0)),
                   pl.BlockSpec((None, N_AB, tm), lambda b, i: (b, 0, i))],
        out_shape=[head_shape, head_shape, head_shape, head_shape,
                   jax.ShapeDtypeStruct((bsz, seq, POOL_WIDTH), BF16),
                   jax.ShapeDtypeStruct((bsz, N_AB, seq), F32)],
        compiler_params=_params(("arbitrary", "arbitrary")),
        name="inproj",
    )(x, x, x, nw, wqkv, wz, wu, wabt, conv_w)


def _block_rows(t, size, parity):
    n = t.shape[0]
    return jnp.concatenate([t[r:r + size] for r in range(parity * size, n, 2 * size)], axis=0)


def _merge_block_rows(t, new_rows, size, parity):
    n = t.shape[0]
    parts = []
    for i, r in enumerate(range(0, n, 2 * size)):
        new = new_rows[i * size:(i + 1) * size]
        old = t[r + (1 - parity) * size:r + (2 - parity) * size]
        parts += [old, new] if parity else [new, old]
    return jnp.concatenate(parts, axis=0)


def _delta_kernel(gp_ref, q_ref, k_ref, v_ref, z_ref, abt_ref, hnw_ref, out_ref,
                  rowb, lvl, sm_s, sn_s, oq_s, o_s):
    n_heads, seq = q_ref.shape[0], q_ref.shape[1]
    nch = seq // CHUNK
    n_lvl = lvl.shape[0]
    ri = lax.broadcasted_iota(I32, (CHUNK, CHUNK), 0)
    ci = lax.broadcasted_iota(I32, (CHUNK, CHUNK), 1)

    def gate_rows(h):
        for d in range(2):
            a_row = abt_ref[2 * d * N_HEADS + h]
            b_row = abt_ref[(2 * d + 1) * N_HEADS + h]
            a_log = jnp.full((1, CHUNK), gp_ref[2 * d, h], F32)
            g_row = -jnp.exp(a_log) * _softplus(a_row + gp_ref[2 * d + 1, h])
            tri = (ri <= ci) if d == 0 else (ri >= ci)
            gc_row = jnp.dot(g_row, tri.astype(F32), precision=lax.Precision.HIGHEST,
                             preferred_element_type=F32)
            g_last = gc_row[:, CHUNK - 1:CHUNK] if d == 0 else gc_row[:, 0:1]
            g_last = jnp.broadcast_to(g_last, (nch, CHUNK))
            rowb[d, 0] = gc_row
            rowb[d, 1] = g_last
            rowb[d, 2] = jnp.exp(g_last)
            rowb[d, 3] = _sigmoid(b_row)

    sel_r = lax.broadcasted_iota(I32, (2 * N_COLQ_PAD, N_COLQ * CHUNK), 0)
    sel_c = lax.broadcasted_iota(I32, (2 * N_COLQ_PAD, N_COLQ * CHUNK), 1)
    col_sel = ((sel_r & (N_COLQ_PAD - 1)) == (sel_c >> (CHUNK.bit_length() - 1))).astype(BF16)
    pad_rows = jnp.zeros((N_COLQ_PAD - N_COLQ, CHUNK), F32)

    for l in range(n_lvl):
        same_pair = (ri >> (l + 1)) == (ci >> (l + 1))
        other_half = (ri >> l) != (ci >> l)
        lvl[l] = (same_pair & other_half).astype(F32)
    eye = (ri == ci).astype(F32)

    def prep_head(h, carry):
        gate_rows(h)
        chains = []
        for c in range(nch):
            r0 = c * CHUNK
            qb = q_ref[h, r0:r0 + CHUNK, :]
            kb = k_ref[h, r0:r0 + CHUNK, :]
            q = qb.astype(F32)
            k = kb.astype(F32)
            v = v_ref[h, r0:r0 + CHUNK, :].astype(F32)
            kk = _dot_nt(kb, kb)
            qk = _dot_nt(qb, kb)
            rows = jnp.concatenate(
                [rowb[0, 0, pl.ds(c, 1), :], rowb[0, 3, pl.ds(c, 1), :],
                 rowb[1, 0, pl.ds(c, 1), :], rowb[1, 3, pl.ds(c, 1), :], pad_rows], axis=0)
            rows_hi = rows.astype(BF16).astype(F32)
            split = jnp.concatenate([rows_hi, rows - rows_hi], axis=0).astype(BF16)
            cols = _dot_tn(split, col_sel)
            for d in range(2):
                incl = (ri >= ci) if d == 0 else (ri <= ci)
                strict = (ri > ci) if d == 0 else (ri < ci)
                gcol = cols[:, 2 * d * CHUNK:(2 * d + 1) * CHUNK]
                grow = rowb[d, 0, pl.ds(c, 1), :]
                beta = cols[:, (2 * d + 1) * CHUNK:(2 * d + 2) * CHUNK]
                decay = jnp.where(incl, jnp.exp(jnp.where(incl, gcol - grow, 0.0)), 0.0)
                a_mat = jnp.where(strict, beta * kk * decay, 0.0)
                egc = jnp.exp(gcol)
                g_last = rowb[d, 1, pl.ds(c, 1), :]
                chains.append(dict(
                    c=c, d=d, r0=r0, a=a_mat,
                    rhs=jnp.concatenate([k * (beta * egc), v * beta], axis=1).astype(BF16),
                    qkm=jnp.where(incl, qk * decay, 0.0).astype(BF16),
                    kd=(k * jnp.exp(g_last - gcol)).astype(BF16),
                    qd=q * egc))
        t_inv = [eye - ch["a"] * lvl[0] for ch in chains]
        for l in range(1, n_lvl):
            size = 1 << l
            t_b = [t.astype(BF16) for t in t_inv]
            c_b = [(ch["a"] * lvl[l]).astype(BF16) for ch in chains]
            if size < SUBLANES:
                xs = [_dot(tb, cb) for tb, cb in zip(t_b, c_b)]
                ys = [_dot(x.astype(BF16), tb) for x, tb in zip(xs, t_b)]
                t_inv = [t - y for t, y in zip(t_inv, ys)]
            else:
                halves = [_block_rows(t, size, 1 - ch["d"]) for t, ch in zip(t_inv, chains)]
                xs = [_dot(hf.astype(BF16), cb) for hf, cb in zip(halves, c_b)]
                ys = [_dot(x.astype(BF16), tb) for x, tb in zip(xs, t_b)]
                t_inv = [_merge_block_rows(t, hf - y, size, 1 - ch["d"])
                         for t, hf, y, ch in zip(t_inv, halves, ys, chains)]
        wus = [_dot(t.astype(BF16), ch["rhs"]).astype(BF16) for t, ch in zip(t_inv, chains)]
        kwus = [_dot_tn(ch["kd"], wu) for ch, wu in zip(chains, wus)]
        qwus = [_dot(ch["qkm"], wu) for ch, wu in zip(chains, wus)]
        for ch, kwu, qwu in zip(chains, kwus, qwus):
            c, d, r0 = ch["c"], ch["d"], ch["r0"]
            gamma = rowb[d, 2, pl.ds(c, 1), :]
            sm_s[h, d, c] = (eye * gamma - kwu[:, :HEAD_DIM]).astype(BF16)
            sn_s[h, d, c] = kwu[:, HEAD_DIM:].astype(BF16)
            oq_s[h, d, r0:r0 + CHUNK, :] = (ch["qd"] - qwu[:, :HEAD_DIM]).astype(BF16)
        for c in range(nch):
            o_s[h, c * CHUNK:(c + 1) * CHUNK, :] = qwus[2 * c][:, HEAD_DIM:] + qwus[2 * c + 1][:, HEAD_DIM:]
        return carry

    lax.fori_loop(0, n_heads, prep_head, 0)

    def scan_step(i, states):
        new_states = []
        for h in range(n_heads):
            for d in range(2):
                c = i if d == 0 else nch - 1 - i
                r0 = pl.multiple_of(c * CHUNK, CHUNK)
                s_b = states[2 * h + d].astype(BF16)
                o_s[h, pl.ds(r0, CHUNK), :] += _dot(oq_s[h, d, pl.ds(r0, CHUNK), :], s_b)
                new_states.append(sn_s[h, d, c].astype(F32) + _dot(sm_s[h, d, c], s_b))
        return tuple(new_states)

    zero_state = jnp.zeros((HEAD_DIM, HEAD_DIM), F32)
    lax.fori_loop(0, nch, scan_step, (zero_state,) * (2 * n_heads), unroll=2)

    for h in range(n_heads):
        o = o_s[h]
        zg = z_ref[h].astype(F32)
        out_ref[:, h * HEAD_DIM:(h + 1) * HEAD_DIM] = (
            _rms(o, hnw_ref[...]) * (zg * _sigmoid(zg))).astype(out_ref.dtype)


def _delta(gate_params, q, k, v, z, abt, hnw):
    bsz, n_heads, seq, _ = q.shape
    nch = seq // CHUNK
    n_lvl = CHUNK.bit_length() - 1
    heads = pl.BlockSpec((None, n_heads, seq, HEAD_DIM), lambda b: (b, 0, 0, 0))
    return pl.pallas_call(
        _delta_kernel,
        grid=(bsz,),
        in_specs=[pl.BlockSpec(memory_space=pltpu.SMEM),
                  heads, heads, heads, heads,
                  pl.BlockSpec((None, N_AB, nch, CHUNK), lambda b: (b, 0, 0, 0)),
                  pl.BlockSpec(hnw.shape, lambda b: (0, 0))],
        out_specs=pl.BlockSpec((None, seq, n_heads * HEAD_DIM), lambda b: (b, 0, 0)),
        out_shape=jax.ShapeDtypeStruct((bsz, seq, n_heads * HEAD_DIM), BF16),
        scratch_shapes=[
            pltpu.VMEM((2, N_COLQ, nch, CHUNK), F32),
            pltpu.VMEM((n_lvl, CHUNK, CHUNK), F32),
            pltpu.VMEM((n_heads, 2, nch, HEAD_DIM, HEAD_DIM), BF16),
            pltpu.VMEM((n_heads, 2, nch, HEAD_DIM, HEAD_DIM), BF16),
            pltpu.VMEM((n_heads, 2, seq, HEAD_DIM), BF16),
            pltpu.VMEM((n_heads, seq, HEAD_DIM), F32),
        ],
        compiler_params=_params(("arbitrary",)),
        name="delta",
    )(gate_params, q, k, v, z, abt, hnw)


def _mixout_kernel(u_ref, od_ref, x_ref, pw_ref, ps_ref, wout_ref, nw_ref, rwt_ref,
                   h_ref, xn_ref, pt_ref, upad):
    seq = u_ref.shape[0]
    tm = od_ref.shape[0]
    i = pl.program_id(1)
    n_tiles = pl.num_programs(1)
    t0 = pl.multiple_of(i * tm, tm)

    upad[POOL_PAD:POOL_PAD + tm, :] = u_ref[pl.ds(t0, tm), :].astype(F32)
    before = u_ref[pl.ds(pl.multiple_of(jnp.maximum(t0 - POOL_PAD, 0), POOL_PAD), POOL_PAD), :].astype(F32)
    after = u_ref[pl.ds(pl.multiple_of(jnp.minimum(t0 + tm, seq - POOL_PAD), POOL_PAD), POOL_PAD), :].astype(F32)
    upad[0:POOL_PAD, :] = jnp.where(i > 0, before, 0.0)
    upad[POOL_PAD + tm:POOL_PAD + tm + POOL_PAD, :] = jnp.where(i < n_tiles - 1, after, 0.0)

    tglob = t0 + lax.broadcasted_iota(I32, (tm, 1), 0)
    pooled_out = []
    for g, window in enumerate(POOL_WINDOWS):
        lo = window // 2
        hi = window - lo - 1
        cols = slice(g * POOL_GROUP_DIM, (g + 1) * POOL_GROUP_DIM)
        total = None
        for dlt in range(-lo, hi + 1):
            term = upad[POOL_PAD + dlt:POOL_PAD + dlt + tm, cols]
            total = term if total is None else total + term
        count = (jnp.minimum(tglob + hi + 1, seq) - jnp.maximum(tglob - lo, 0)).astype(F32)
        diff = total / count - upad[POOL_PAD:POOL_PAD + tm, cols]
        pooled_out.append(_dot(diff.astype(BF16), pw_ref[g]))
    o_pool = jnp.concatenate(pooled_out, axis=1) * ps_ref[...]

    h = (x_ref[...]
         + _dot(od_ref[...], wout_ref[0:DELTA_WIDTH, :])
         + _dot(o_pool.astype(BF16), wout_ref[DELTA_WIDTH:DELTA_WIDTH + POOL_WIDTH, :]))
    h_ref[...] = h
    xn = _rms(h, nw_ref[...]).astype(BF16)
    xn_ref[...] = xn
    logits_t = _dot_nt(rwt_ref[...], xn)
    e_t = jnp.exp(logits_t - jnp.max(logits_t, axis=0, keepdims=True))
    pt_ref[...] = e_t / jnp.sum(e_t, axis=0, keepdims=True)


def _mixout(u, od, x, pw, ps, wout, nw, rw, tm=512):
    bsz, seq, _ = x.shape
    rwt = rw.T
    full = lambda a: pl.BlockSpec(a.shape, lambda b, i: (0,) * a.ndim)
    tile = lambda n: pl.BlockSpec((None, tm, n), lambda b, i: (b, i, 0))
    return pl.pallas_call(
        _mixout_kernel,
        grid=(bsz, seq // tm),
        in_specs=[pl.BlockSpec((None, seq, POOL_WIDTH), lambda b, i: (b, 0, 0)),
                  tile(DELTA_WIDTH), tile(D_MODEL), full(pw), full(ps), full(wout), full(nw), full(rwt)],
        out_specs=[tile(D_MODEL), tile(D_MODEL),
                   pl.BlockSpec((None, N_EXPERTS, tm), lambda b, i: (b, 0, i))],
        out_shape=[jax.ShapeDtypeStruct((bsz, seq, D_MODEL), F32),
                   jax.ShapeDtypeStruct((bsz, seq, D_MODEL), BF16),
                   jax.ShapeDtypeStruct((bsz, N_EXPERTS, seq), F32)],
        scratch_shapes=[pltpu.VMEM((tm + 2 * POOL_PAD, POOL_WIDTH), F32)],
        compiler_params=_params(("arbitrary", "arbitrary")),
        name="mixout",
    )(u, od, x, pw, ps, wout, nw, rwt)


def _route_kernel(pt_ref, slott_ref, slot_ref, cnt_ref, *, cap):
    p = pt_ref[...]
    rows, seq = p.shape
    n_e = slot_ref.shape[2]

    def bracket(mid_fn):
        def step(_, bounds):
            lo, hi = bounds
            mid = mid_fn(lo, hi)
            enough = jnp.sum((p >= mid).astype(F32), axis=1, keepdims=True) >= cap
            return jnp.where(enough, mid, lo), jnp.where(enough, hi, mid)
        return step

    bounds = (jnp.zeros((rows, 1), F32), jnp.full((rows, 1), 2.0, F32))
    bounds = lax.fori_loop(0, N_BISECT_GEO, bracket(lambda lo, hi: jnp.sqrt(jnp.maximum(lo, F32_TINY) * hi)),
                           bounds)
    lo, hi = lax.fori_loop(0, N_BISECT_LIN, bracket(lambda lo, hi: 0.5 * (lo + hi)), bounds)

    above = p >= hi
    tied = (p >= lo) & (p < hi)
    need = cap - jnp.sum(above.astype(F32), axis=1, keepdims=True)
    ri = lax.broadcasted_iota(I32, (CHUNK, CHUNK), 0)
    ci = lax.broadcasted_iota(I32, (CHUNK, CHUNK), 1)
    tri_b = (ri <= ci).astype(BF16)

    def prefix_count(mask):
        carry = jnp.zeros((rows, 1), F32)
        blocks = []
        for c0 in range(0, seq, CHUNK):
            part = _dot(mask[:, c0:c0 + CHUNK].astype(BF16), tri_b) + carry
            blocks.append(part)
            carry = part[:, CHUNK - 1:CHUNK]
        return jnp.concatenate(blocks, axis=1)

    tied_f = tied.astype(F32)
    sel = above | (tied & (prefix_count(tied_f) - tied_f < need))
    pos = prefix_count(sel.astype(F32))
    slot_t = jnp.where(sel, pos - 1.0, -1.0)
    slott_ref[...] = slot_t.astype(I32)
    tt = seq // cnt_ref.shape[1]
    cnt_ref[...] = jnp.concatenate([pos[:, t1 - 1:t1] for t1 in range(tt, seq + 1, tt)], axis=1).astype(I32)
    eye_b = (lax.broadcasted_iota(I32, (n_e, n_e), 0) == lax.broadcasted_iota(I32, (n_e, n_e), 1)).astype(BF16)
    for b in range(rows // n_e):
        slot_ref[b] = _dot_tn(slot_t[b * n_e:(b + 1) * n_e, :].astype(BF16), eye_b).astype(I32)


def _route(probs_t, cap):
    bsz, n_e, seq = probs_t.shape
    rows = bsz * n_e
    n_tiles = seq // COMBINE_TILE
    slot_t, slot, tile_cnt = pl.pallas_call(
        functools.partial(_route_kernel, cap=cap),
        grid=(1,),
        in_specs=[pl.BlockSpec((rows, seq), lambda i: (0, 0))],
        out_specs=[pl.BlockSpec((rows, seq), lambda i: (0, 0)),
                   pl.BlockSpec((bsz, seq, n_e), lambda i: (0, 0, 0)),
                   pl.BlockSpec((rows, n_tiles), lambda i: (0, 0))],
        out_shape=[jax.ShapeDtypeStruct((rows, seq), I32),
                   jax.ShapeDtypeStruct((bsz, seq, n_e), I32),
                   jax.ShapeDtypeStruct((rows, n_tiles), I32)],
        compiler_params=_params(("arbitrary",)),
        name="route",
    )(probs_t.reshape(rows, seq))
    return slot_t.reshape(bsz, n_e, seq), slot, tile_cnt


def _slot_window(cnt_ref, row, tile, cap):
    hi = cnt_ref[row, tile]
    lo = jnp.where(tile > 0, cnt_ref[row, jnp.maximum(tile - 1, 0)], 0)
    base = jnp.minimum(lo - (lo & (BF16_ROWS - 1)), cap - SLOT_WINDOW)
    return base, hi - base > SLOT_WINDOW


def _gather_kernel(cnt_ref, slott_ref, pt_ref, xn_ref, xg_ref, gate_ref):
    n_e, cap = xg_ref.shape[0], xg_ref.shape[1]
    b, tile = pl.program_id(0), pl.program_id(1)

    @pl.when(tile == 0)
    def _():
        xg_ref[...] = jnp.zeros(xg_ref.shape, BF16)
        gate_ref[...] = jnp.zeros(gate_ref.shape, F32)

    slot_t = slott_ref[...].astype(F32)
    probs_t = pt_ref[...]
    xn = xn_ref[...]
    tt = slot_t.shape[1]
    packed = PACK * SLOT_WINDOW
    row_in = (lax.broadcasted_iota(I32, (packed, 1), 0) & (SLOT_WINDOW - 1)).astype(F32)
    overflows = []
    for group in range(n_e // PACK):
        targets, gates, bases = [], [], []
        for j in range(PACK):
            e = group * PACK + j
            base, overflow = _slot_window(cnt_ref, b * n_e + e, tile, cap)
            base_f = base.astype(F32)
            end_f = jnp.where(overflow, base_f, base_f + SLOT_WINDOW)
            srow = slot_t[e:e + 1, :]
            local = jnp.where((srow >= base_f) & (srow < end_f), srow - base_f, -1.0)
            targets.append(jnp.broadcast_to(local, (SLOT_WINDOW, tt)))
            gates.append(jnp.broadcast_to(probs_t[e:e + 1, :], (SLOT_WINDOW, tt)))
            bases.append(base)
            overflows.append(overflow)
        hit = jnp.concatenate(targets, axis=0) == row_in
        rows = _dot(hit.astype(BF16), xn)
        gate = jnp.sum(jnp.where(hit, jnp.concatenate(gates, axis=0), 0.0), axis=1, keepdims=True)
        for j in range(PACK):
            e = group * PACK + j
            win = pl.ds(pl.multiple_of(bases[j], BF16_ROWS), SLOT_WINDOW)
            part = slice(j * SLOT_WINDOW, (j + 1) * SLOT_WINDOW)
            xg_ref[e, win, :] = (xg_ref[e, win, :].astype(F32) + rows[part]).astype(BF16)
            gate_ref[e, win, :] += gate[part]
    all_slots = lax.broadcasted_iota(I32, (cap, 1), 0).astype(F32)
    for e in range(n_e):

        @pl.when(overflows[e])
        def _():
            hit = slott_ref[e:e + 1, :].astype(F32) == all_slots
            xg_ref[e] = (xg_ref[e].astype(F32) + _dot(hit.astype(BF16), xn_ref[...])).astype(BF16)
            gate_ref[e] += jnp.sum(jnp.where(hit, pt_ref[e:e + 1, :], 0.0), axis=1, keepdims=True)


def _gather(tile_cnt, slot_t, probs_t, xn, cap):
    bsz, n_e, seq = slot_t.shape
    tt = COMBINE_TILE
    rows = pl.BlockSpec((None, n_e, tt), lambda b, t, cnt: (b, 0, t))
    return pl.pallas_call(
        _gather_kernel,
        grid_spec=pltpu.PrefetchScalarGridSpec(
            num_scalar_prefetch=1,
            grid=(bsz, seq // tt),
            in_specs=[rows, rows, pl.BlockSpec((None, tt, D_MODEL), lambda b, t, cnt: (b, t, 0))],
            out_specs=[pl.BlockSpec((n_e, None, cap, D_MODEL), lambda b, t, cnt: (0, b, 0, 0)),
                       pl.BlockSpec((n_e, None, cap, 1), lambda b, t, cnt: (0, b, 0, 0))]),
        out_shape=[jax.ShapeDtypeStruct((n_e, bsz, cap, D_MODEL), BF16),
                   jax.ShapeDtypeStruct((n_e, bsz, cap, 1), F32)],
        compiler_params=_params(("arbitrary", "arbitrary")),
        name="gather",
    )(tile_cnt, slot_t, probs_t, xn)


def _ffn_kernel(xg_ref, gate_ref, wg_ref, wu_ref, wd_ref, y_ref, acc_ref, wgb, wub, wdb, *, tr):
    f = pl.program_id(1)
    last = pl.num_programs(1) - 1
    n_blocks = xg_ref.shape[0] // tr

    def gate_up(r):
        x = xg_ref[r * tr:(r + 1) * tr, :]
        return _dot(x, wgb[...]), _dot(x, wub[...])

    def ff_tile(first, final):
        wgb[...] = wg_ref[...].astype(BF16)
        gate0 = _dot(xg_ref[0:tr, :], wgb[...])
        wub[...] = wu_ref[...].astype(BF16)
        nxt = (gate0, _dot(xg_ref[0:tr, :], wub[...]))
        wdb[...] = wd_ref[...].astype(BF16)
        for r in range(n_blocks):
            gate, up = nxt
            if r + 1 < n_blocks:
                nxt = gate_up(r + 1)
            rows = slice(r * tr, (r + 1) * tr)
            hidden = (gate * _sigmoid(gate) * up).astype(BF16)
            total = _dot(hidden, wdb[...])
            if not first:
                total = total + acc_ref[rows, :]
            if final:
                y_ref[rows, :] = (total * gate_ref[rows, :]).astype(BF16)
            else:
                acc_ref[rows, :] = total

    pl.when(f == 0)(functools.partial(ff_tile, True, False))
    pl.when((f > 0) & (f < last))(functools.partial(ff_tile, False, False))
    pl.when(f == last)(functools.partial(ff_tile, False, True))


def _ffn(xg, gates, wg, wu, wd, tf=512, tr=512):
    n_e, rows, _ = xg.shape
    ff = wg.shape[2]
    return pl.pallas_call(
        functools.partial(_ffn_kernel, tr=tr),
        grid=(n_e, ff // tf),
        in_specs=[pl.BlockSpec((None, rows, D_MODEL), lambda e, f: (e, 0, 0)),
                  pl.BlockSpec((None, rows, 1), lambda e, f: (e, 0, 0)),
                  pl.BlockSpec((None, D_MODEL, tf), lambda e, f: (e, 0, f)),
                  pl.BlockSpec((None, D_MODEL, tf), lambda e, f: (e, 0, f)),
                  pl.BlockSpec((None, tf, D_MODEL), lambda e, f: (e, f, 0))],
        out_specs=pl.BlockSpec((None, rows, D_MODEL), lambda e, f: (e, 0, 0)),
        out_shape=jax.ShapeDtypeStruct((n_e, rows, D_MODEL), BF16),
        scratch_shapes=[pltpu.VMEM((rows, D_MODEL), F32),
                        pltpu.VMEM((D_MODEL, tf), BF16),
                        pltpu.VMEM((D_MODEL, tf), BF16),
                        pltpu.VMEM((tf, D_MODEL), BF16)],
        compiler_params=_params(("arbitrary", "arbitrary")),
        name="ffn",
    )(xg, gates, wg, wu, wd)


def _combine_kernel(cnt_ref, slot_ref, y_ref, h_ref, nw_ref, out_ref):
    n_e, cap = y_ref.shape[0], y_ref.shape[1]
    b, tile = pl.program_id(0), pl.program_id(1)
    slot = slot_ref[...].astype(F32)
    packed = PACK * SLOT_WINDOW
    lane = lax.broadcasted_iota(I32, (1, packed), 1)
    lane_f = lane.astype(F32)
    all_slots = lax.broadcasted_iota(I32, (1, cap), 1).astype(F32)
    acc = h_ref[...]
    overflows = []
    for group in range(n_e // PACK):
        target = jnp.full((1, packed), -1.0, F32)
        windows = []
        for j in range(PACK):
            e = group * PACK + j
            base, overflow = _slot_window(cnt_ref, b * n_e + e, tile, cap)
            base_f = base.astype(F32)
            end_f = jnp.where(overflow, base_f, base_f + SLOT_WINDOW)
            col = slot[:, e:e + 1]
            inside = (col >= base_f) & (col < end_f)
            local = jnp.where(inside, col - base_f + j * SLOT_WINDOW, -1.0)
            target = jnp.where((lane >> (SLOT_WINDOW.bit_length() - 1)) == j, local, target)
            windows.append(y_ref[e, pl.ds(pl.multiple_of(base, BF16_ROWS), SLOT_WINDOW), :])
            overflows.append(overflow)
        onehot = (target == lane_f).astype(BF16)
        acc = acc + _dot(onehot, jnp.concatenate(windows, axis=0))
    out_ref[...] = acc
    for e in range(n_e):

        @pl.when(overflows[e])
        def _():
            full = (slot_ref[:, e:e + 1].astype(F32) == all_slots).astype(BF16)
            out_ref[...] += _dot(full, y_ref[e])

    out_ref[...] = _rms(out_ref[...], nw_ref[...])


def _combine(tile_cnt, slot, y, h, nw):
    bsz, seq, n_e = slot.shape
    cap = y.shape[2]
    tt = COMBINE_TILE
    tok = lambda n: pl.BlockSpec((None, tt, n), lambda b, t, cnt: (b, t, 0))
    return pl.pallas_call(
        _combine_kernel,
        grid_spec=pltpu.PrefetchScalarGridSpec(
            num_scalar_prefetch=1,
            grid=(bsz, seq // tt),
            in_specs=[tok(n_e),
                      pl.BlockSpec((n_e, None, cap, D_MODEL), lambda b, t, cnt: (0, b, 0, 0)),
                      tok(D_MODEL),
                      pl.BlockSpec(nw.shape, lambda b, t, cnt: (0, 0))],
            out_specs=tok(D_MODEL)),
        out_shape=jax.ShapeDtypeStruct((bsz, seq, D_MODEL), F32),
        compiler_params=_params(("arbitrary", "arbitrary")),
        name="combine",
    )(tile_cnt, slot, y, h, nw)


def _layer(x, norm_mix_w, w_in, conv_w, a_log_fwd, dt_bias_fwd, a_log_bwd, dt_bias_bwd,
           head_norm_w, pool_w, pool_scale, w_out, norm_ffn_w, router_w,
           expert_w_gate, expert_w_up, expert_w_down, norm_final_w):
    bsz, seq, _ = x.shape
    nch = seq // CHUNK
    cap = EC_CAPACITY * seq // N_EXPERTS
    c_z = 3 * DELTA_WIDTH
    c_ab = c_z + DELTA_WIDTH
    c_u = c_ab + N_AB
    w_bf = w_in.astype(BF16)
    q, k, v, z, u, abt = _inproj(
        x, norm_mix_w.reshape(1, D_MODEL),
        w_bf[:, :c_z], w_bf[:, c_z:c_ab], w_bf[:, c_u:], w_bf[:, c_ab:c_u].T, conv_w)

    gate_params = jnp.stack([a_log_fwd, dt_bias_fwd, a_log_bwd, dt_bias_bwd]).astype(F32)
    o_delta = _delta(gate_params, q, k, v, z,
                     abt.reshape(bsz, N_AB, nch, CHUNK), head_norm_w.reshape(1, HEAD_DIM))

    h, xn, probs_t = _mixout(u, o_delta, x,
                             pool_w.astype(BF16), pool_scale.reshape(1, POOL_WIDTH),
                             w_out.astype(BF16), norm_ffn_w.reshape(1, D_MODEL), router_w.astype(BF16))
    slot_t, slot, tile_cnt = _route(probs_t, cap)
    xg, gates = _gather(tile_cnt, slot_t, probs_t, xn, cap)
    y = _ffn(xg.reshape(N_EXPERTS, bsz * cap, D_MODEL), gates.reshape(N_EXPERTS, bsz * cap, 1),
             expert_w_gate, expert_w_up, expert_w_down)
    return _combine(tile_cnt, slot, y.reshape(N_EXPERTS, bsz, cap, D_MODEL), h,
                    norm_final_w.reshape(1, D_MODEL))


def kernel(x, norm_mix_w, w_in, conv_w, a_log_fwd, dt_bias_fwd, a_log_bwd, dt_bias_bwd,
           head_norm_w, pool_w, pool_scale, w_out, norm_ffn_w, router_w,
           expert_w_gate, expert_w_up, expert_w_down, norm_final_w):
    assert w_in.shape[0] == 1, "single-layer stack: the final norm is fused into the layer's last kernel"
    first = lambda a: a.reshape(a.shape[1:])
    return _layer(x, first(norm_mix_w), first(w_in), first(conv_w), first(a_log_fwd), first(dt_bias_fwd),
                  first(a_log_bwd), first(dt_bias_bwd), first(head_norm_w), first(pool_w),
                  first(pool_scale), first(w_out), first(norm_ffn_w), first(router_w),
                  first(expert_w_gate), first(expert_w_up), first(expert_w_down), norm_final_w)
```

```python
import functools

import jax
import jax.numpy as jnp
from jax import lax
from jax.experimental import pallas as pl
from jax.experimental.pallas import tpu as pltpu

F32 = jnp.float32
BF16 = jnp.bfloat16
I32 = jnp.int32

D_MODEL = 1024
N_HEADS = 4
HEAD_DIM = 128
DELTA_WIDTH = N_HEADS * HEAD_DIM
POOL_WINDOWS = (2, 4, 8, 16)
POOL_GROUP_DIM = 128
POOL_WIDTH = len(POOL_WINDOWS) * POOL_GROUP_DIM
SHORT_CONV = 5
N_EXPERTS = 16
EC_CAPACITY = 2
EXPERT_FF = 2 * D_MODEL
RMS_EPS = 1e-6

MXU_COLS = 256
SUBLANES = 8
CHUNK = 128
N_BISECT_GEO = 34
N_BISECT_LIN = 6
F32_TINY = 2.0 ** -126
BF16_ROWS = 16
COMBINE_TILE = 256
TILES_PER_STEP = 2
SLOT_WINDOW = 64
PACK = 4
HALO = 16
POOL_PAD = 16
N_AB = 4 * N_HEADS
N_COLQ = 4
N_COLQ_PAD = 8
VMEM_LIMIT = 56 * 1024 * 1024


def _sigmoid(x):
    return 1.0 / (1.0 + jnp.exp(-x))


def _softplus(x):
    return jnp.maximum(x, 0.0) + jnp.log(1.0 + jnp.exp(-jnp.abs(x)))


def _dot(a, b):
    return jnp.dot(a, b, preferred_element_type=F32)


def _dot_nt(a, b):
    return lax.dot_general(a, b, (((1,), (1,)), ((), ())), preferred_element_type=F32)


def _dot_tn(a, b):
    return lax.dot_general(a, b, (((0,), (0,)), ((), ())), preferred_element_type=F32)


def _rms(x, w):
    return x * lax.rsqrt(jnp.mean(x * x, axis=-1, keepdims=True) + RMS_EPS) * w


def _params(sem):
    return pltpu.CompilerParams(dimension_semantics=sem, vmem_limit_bytes=VMEM_LIMIT)


def _inproj_kernel(x_ref, xprev_ref, xnext_ref, nw_ref, wqkv_ref, wz_ref, wu_ref, wabt_ref, cw_ref,
                   q_ref, k_ref, v_ref, z_ref, u_ref, abt_ref):
    tm = x_ref.shape[0]
    i = pl.program_id(1)
    nw = nw_ref[...]
    x_ext = jnp.concatenate([xprev_ref[...], x_ref[...], xnext_ref[...]], axis=0)
    xn_ext = _rms(x_ext, nw).astype(BF16)
    xn = xn_ext[HALO:HALO + tm, :]
    n_ext = tm + 2 * HALO
    keep_before = jnp.where(i > 0, 1.0, 0.0)
    keep_after = jnp.where(i < pl.num_programs(1) - 1, 1.0, 0.0)

    def project(pair):
        cols = slice(pair * MXU_COLS, (pair + 1) * MXU_COLS)
        res = _dot(xn_ext, wqkv_ref[:, cols])
        return jnp.concatenate([res[0:HALO] * keep_before, res[HALO:HALO + tm],
                                res[HALO + tm:n_ext] * keep_after], axis=0)

    def conv_act(blk, ext):
        cols = slice(blk * HEAD_DIM, (blk + 1) * HEAD_DIM)
        acc = None
        for j in range(SHORT_CONV):
            shift = (SHORT_CONV // 2 - j) % n_ext
            rolled = ext if shift == 0 else pltpu.roll(ext, shift, 0)
            term = rolled[HALO:HALO + tm] * cw_ref[j:j + 1, cols]
            acc = term if acc is None else acc + term
        act = acc * _sigmoid(acc)
        kind, head = divmod(blk, N_HEADS)
        if kind == 0:
            scale = lax.rsqrt(jnp.sum(act * act, axis=-1, keepdims=True) + RMS_EPS) * (HEAD_DIM ** -0.5)
            q_ref[head] = (act * scale).astype(BF16)
        elif kind == 1:
            scale = lax.rsqrt(jnp.sum(act * act, axis=-1, keepdims=True) + RMS_EPS)
            k_ref[head] = (act * scale).astype(BF16)
        else:
            v_ref[head] = act.astype(BF16)

    n_pairs = 3 * DELTA_WIDTH // MXU_COLS
    heads_per_pair = MXU_COLS // HEAD_DIM
    nxt = project(0)
    for pair in range(n_pairs):
        cur = nxt
        if pair + 1 < n_pairs:
            nxt = project(pair + 1)
        else:
            z = _dot(xn, wz_ref[...])
            for head in range(N_HEADS):
                z_ref[head] = z[:, head * HEAD_DIM:(head + 1) * HEAD_DIM].astype(BF16)
            u_ref[...] = _dot(xn, wu_ref[...]).astype(BF16)
            abt_ref[...] = _dot_nt(wabt_ref[...], xn)
        for sub in range(heads_per_pair):
            conv_act(pair * heads_per_pair + sub, cur[:, sub * HEAD_DIM:(sub + 1) * HEAD_DIM])


def _inproj(x, nw, wqkv, wz, wu, wabt, conv_w, tm=512):
    bsz, seq, _ = x.shape
    full = lambda a: pl.BlockSpec(a.shape, lambda b, i: (0, 0))
    heads = pl.BlockSpec((None, N_HEADS, tm, HEAD_DIM), lambda b, i: (b, 0, i, 0))
    halo_per_tile = tm // HALO
    head_shape = jax.ShapeDtypeStruct((bsz, N_HEADS, seq, HEAD_DIM), BF16)
    return pl.pallas_call(
        _inproj_kernel,
        grid=(bsz, seq // tm),
        in_specs=[pl.BlockSpec((None, tm, D_MODEL), lambda b, i: (b, i, 0)),
                  pl.BlockSpec((None, HALO, D_MODEL),
                               lambda b, i: (b, jnp.maximum(i * halo_per_tile - 1, 0), 0)),
                  pl.BlockSpec((None, HALO, D_MODEL),
                               lambda b, i: (b, jnp.minimum((i + 1) * halo_per_tile, seq // HALO - 1), 0)),
                  full(nw), full(wqkv), full(wz), full(wu), full(wabt), full(conv_w)],
        out_specs=[heads, heads, heads, heads,
                   pl.BlockSpec((None, tm, POOL_WIDTH), lambda b, i: (b, i, 0)),
                   pl.BlockSpec((None, N_AB, tm), lambda b, i: (b, 0, i))],
        out_shape=[head_shape, head_shape, head_shape, head_shape,
                   jax.ShapeDtypeStruct((bsz, seq, POOL_WIDTH), BF16),
                   jax.ShapeDtypeStruct((bsz, N_AB, seq), F32)],
        compiler_params=_params(("arbitrary", "arbitrary")),
        name="inproj",
    )(x, x, x, nw, wqkv, wz, wu, wabt, conv_w)


def _block_rows(t, size, parity):
    n = t.shape[0]
    return jnp.concatenate([t[r:r + size] for r in range(parity * size, n, 2 * size)], axis=0)


def _merge_block_rows(t, new_rows, size, parity):
    n = t.shape[0]
    parts = []
    for i, r in enumerate(range(0, n, 2 * size)):
        new = new_rows[i * size:(i + 1) * size]
        old = t[r + (1 - parity) * size:r + (2 - parity) * size]
        parts += [old, new] if parity else [new, old]
    return jnp.concatenate(parts, axis=0)


def _delta_kernel(gp_ref, q_ref, k_ref, v_ref, z_ref, abt_ref, hnw_ref, out_ref,
                  rowb, lvl, sm_s, sn_s, oq_s, o_s):
    n_heads, seq = q_ref.shape[0], q_ref.shape[1]
    nch = seq // CHUNK
    n_lvl = lvl.shape[0]
    ri = lax.broadcasted_iota(I32, (CHUNK, CHUNK), 0)
    ci = lax.broadcasted_iota(I32, (CHUNK, CHUNK), 1)

    def gate_rows(h):
        for d in range(2):
            a_row = abt_ref[2 * d * N_HEADS + h]
            b_row = abt_ref[(2 * d + 1) * N_HEADS + h]
            a_log = jnp.full((1, CHUNK), gp_ref[2 * d, h], F32)
            g_row = -jnp.exp(a_log) * _softplus(a_row + gp_ref[2 * d + 1, h])
            tri = (ri <= ci) if d == 0 else (ri >= ci)
            gc_row = jnp.dot(g_row, tri.astype(F32), precision=lax.Precision.HIGHEST,
                             preferred_element_type=F32)
            g_last = gc_row[:, CHUNK - 1:CHUNK] if d == 0 else gc_row[:, 0:1]
            g_last = jnp.broadcast_to(g_last, (nch, CHUNK))
            rowb[d, 0] = gc_row
            rowb[d, 1] = g_last
            rowb[d, 2] = jnp.exp(g_last)
            rowb[d, 3] = _sigmoid(b_row)

    sel_r = lax.broadcasted_iota(I32, (2 * N_COLQ_PAD, N_COLQ * CHUNK), 0)
    sel_c = lax.broadcasted_iota(I32, (2 * N_COLQ_PAD, N_COLQ * CHUNK), 1)
    col_sel = ((sel_r & (N_COLQ_PAD - 1)) == (sel_c >> (CHUNK.bit_length() - 1))).astype(BF16)
    pad_rows = jnp.zeros((N_COLQ_PAD - N_COLQ, CHUNK), F32)

    for l in range(n_lvl):
        same_pair = (ri >> (l + 1)) == (ci >> (l + 1))
        other_half = (ri >> l) != (ci >> l)
        lvl[l] = (same_pair & other_half).astype(F32)
    eye = (ri == ci).astype(F32)

    def prep_head(h, carry):
        gate_rows(h)
        chains = []
        for c in range(nch):
            r0 = c * CHUNK
            qb = q_ref[h, r0:r0 + CHUNK, :]
            kb = k_ref[h, r0:r0 + CHUNK, :]
            q = qb.astype(F32)
            k = kb.astype(F32)
            v = v_ref[h, r0:r0 + CHUNK, :].astype(F32)
            kk = _dot_nt(kb, kb)
            qk = _dot_nt(qb, kb)
            rows = jnp.concatenate(
                [rowb[0, 0, pl.ds(c, 1), :], rowb[0, 3, pl.ds(c, 1), :],
                 rowb[1, 0, pl.ds(c, 1), :], rowb[1, 3, pl.ds(c, 1), :], pad_rows], axis=0)
            rows_hi = rows.astype(BF16).astype(F32)
            split = jnp.concatenate([rows_hi, rows - rows_hi], axis=0).astype(BF16)
            cols = _dot_tn(split, col_sel)
            for d in range(2):
                incl = (ri >= ci) if d == 0 else (ri <= ci)
                strict = (ri > ci) if d == 0 else (ri < ci)
                gcol = cols[:, 2 * d * CHUNK:(2 * d + 1) * CHUNK]
                grow = rowb[d, 0, pl.ds(c, 1), :]
                beta = cols[:, (2 * d + 1) * CHUNK:(2 * d + 2) * CHUNK]
                decay = jnp.where(incl, jnp.exp(jnp.where(incl, gcol - grow, 0.0)), 0.0)
                a_mat = jnp.where(strict, beta * kk * decay, 0.0)
                egc = jnp.exp(gcol)
                g_last = rowb[d, 1, pl.ds(c, 1), :]
                chains.append(dict(
                    c=c, d=d, r0=r0, a=a_mat,
                    rhs=jnp.concatenate([k * (beta * egc), v * beta], axis=1).astype(BF16),
                    qkm=jnp.where(incl, qk * decay, 0.0).astype(BF16),
                    kd=(k * jnp.exp(g_last - gcol)).astype(BF16),
                    qd=q * egc))
        t_inv = [eye - ch["a"] * lvl[0] for ch in chains]
        for l in range(1, n_lvl):
            size = 1 << l
            t_b = [t.astype(BF16) for t in t_inv]
            c_b = [(ch["a"] * lvl[l]).astype(BF16) for ch in chains]
            if size < SUBLANES:
                xs = [_dot(tb, cb) for tb, cb in zip(t_b, c_b)]
                ys = [_dot(x.astype(BF16), tb) for x, tb in zip(xs, t_b)]
                t_inv = [t - y for t, y in zip(t_inv, ys)]
            else:
                halves = [_block_rows(t, size, 1 - ch["d"]) for t, ch in zip(t_inv, chains)]
                xs = [_dot(hf.astype(BF16), cb) for hf, cb in zip(halves, c_b)]
                ys = [_dot(x.astype(BF16), tb) for x, tb in zip(xs, t_b)]
                t_inv = [_merge_block_rows(t, hf - y, size, 1 - ch["d"])
                         for t, hf, y, ch in zip(t_inv, halves, ys, chains)]
        wus = [_dot(t.astype(BF16), ch["rhs"]).astype(BF16) for t, ch in zip(t_inv, chains)]
        kwus = [_dot_tn(ch["kd"], wu) for ch, wu in zip(chains, wus)]
        qwus = [_dot(ch["qkm"], wu) for ch, wu in zip(chains, wus)]
        for ch, kwu, qwu in zip(chains, kwus, qwus):
            c, d, r0 = ch["c"], ch["d"], ch["r0"]
            gamma = rowb[d, 2, pl.ds(c, 1), :]
            sm_s[h, d, c] = (eye * gamma - kwu[:, :HEAD_DIM]).astype(BF16)
            sn_s[h, d, c] = kwu[:, HEAD_DIM:].astype(BF16)
            oq_s[h, d, r0:r0 + CHUNK, :] = (ch["qd"] - qwu[:, :HEAD_DIM]).astype(BF16)
        for c in range(nch):
            o_s[h, c * CHUNK:(c + 1) * CHUNK, :] = qwus[2 * c][:, HEAD_DIM:] + qwus[2 * c + 1][:, HEAD_DIM:]
        return carry

    lax.fori_loop(0, n_heads, prep_head, 0)

    def scan_step(i, states):
        new_states = []
        for h in range(n_heads):
            for d in range(2):
                c = i if d == 0 else nch - 1 - i
                r0 = pl.multiple_of(c * CHUNK, CHUNK)
                s_b = states[2 * h + d].astype(BF16)
                o_s[h, pl.ds(r0, CHUNK), :] += _dot(oq_s[h, d, pl.ds(r0, CHUNK), :], s_b)
                new_states.append(sn_s[h, d, c].astype(F32) + _dot(sm_s[h, d, c], s_b))
        return tuple(new_states)

    zero_state = jnp.zeros((HEAD_DIM, HEAD_DIM), F32)
    lax.fori_loop(0, nch, scan_step, (zero_state,) * (2 * n_heads), unroll=2)

    for h in range(n_heads):
        o = o_s[h]
        zg = z_ref[h].astype(F32)
        out_ref[:, h * HEAD_DIM:(h + 1) * HEAD_DIM] = (
            _rms(o, hnw_ref[...]) * (zg * _sigmoid(zg))).astype(out_ref.dtype)


def _delta(gate_params, q, k, v, z, abt, hnw):
    bsz, n_heads, seq, _ = q.shape
    nch = seq // CHUNK
    n_lvl = CHUNK.bit_length() - 1
    heads = pl.BlockSpec((None, n_heads, seq, HEAD_DIM), lambda b: (b, 0, 0, 0))
    return pl.pallas_call(
        _delta_kernel,
        grid=(bsz,),
        in_specs=[pl.BlockSpec(memory_space=pltpu.SMEM),
                  heads, heads, heads, heads,
                  pl.BlockSpec((None, N_AB, nch, CHUNK), lambda b: (b, 0, 0, 0)),
                  pl.BlockSpec(hnw.shape, lambda b: (0, 0))],
        out_specs=pl.BlockSpec((None, seq, n_heads * HEAD_DIM), lambda b: (b, 0, 0)),
        out_shape=jax.ShapeDtypeStruct((bsz, seq, n_heads * HEAD_DIM), BF16),
        scratch_shapes=[
            pltpu.VMEM((2, N_COLQ, nch, CHUNK), F32),
            pltpu.VMEM((n_lvl, CHUNK, CHUNK), F32),
            pltpu.VMEM((n_heads, 2, nch, HEAD_DIM, HEAD_DIM), BF16),
            pltpu.VMEM((n_heads, 2, nch, HEAD_DIM, HEAD_DIM), BF16),
            pltpu.VMEM((n_heads, 2, seq, HEAD_DIM), BF16),
            pltpu.VMEM((n_heads, seq, HEAD_DIM), F32),
        ],
        compiler_params=_params(("arbitrary",)),
        name="delta",
    )(gate_params, q, k, v, z, abt, hnw)


def _mixout_kernel(u_ref, od_ref, x_ref, pw_ref, ps_ref, wout_ref, nw_ref, rwt_ref,
                   h_ref, xn_ref, pt_ref, upad):
    seq = u_ref.shape[0]
    tm = od_ref.shape[0]
    i = pl.program_id(1)
    n_tiles = pl.num_programs(1)
    t0 = pl.multiple_of(i * tm, tm)

    upad[POOL_PAD:POOL_PAD + tm, :] = u_ref[pl.ds(t0, tm), :].astype(F32)
    before = u_ref[pl.ds(pl.multiple_of(jnp.maximum(t0 - POOL_PAD, 0), POOL_PAD), POOL_PAD), :].astype(F32)
    after = u_ref[pl.ds(pl.multiple_of(jnp.minimum(t0 + tm, seq - POOL_PAD), POOL_PAD), POOL_PAD), :].astype(F32)
    upad[0:POOL_PAD, :] = jnp.where(i > 0, before, 0.0)
    upad[POOL_PAD + tm:POOL_PAD + tm + POOL_PAD, :] = jnp.where(i < n_tiles - 1, after, 0.0)

    tglob = t0 + lax.broadcasted_iota(I32, (tm, 1), 0)
    pooled_out = []
    for g, window in enumerate(POOL_WINDOWS):
        lo = window // 2
        hi = window - lo - 1
        cols = slice(g * POOL_GROUP_DIM, (g + 1) * POOL_GROUP_DIM)
        total = None
        for dlt in range(-lo, hi + 1):
            term = upad[POOL_PAD + dlt:POOL_PAD + dlt + tm, cols]
            total = term if total is None else total + term
        count = (jnp.minimum(tglob + hi + 1, seq) - jnp.maximum(tglob - lo, 0)).astype(F32)
        diff = total / count - upad[POOL_PAD:POOL_PAD + tm, cols]
        pooled_out.append(_dot(diff.astype(BF16), pw_ref[g]))
    o_pool = jnp.concatenate(pooled_out, axis=1) * ps_ref[...]

    h = (x_ref[...]
         + _dot(od_ref[...], wout_ref[0:DELTA_WIDTH, :])
         + _dot(o_pool.astype(BF16), wout_ref[DELTA_WIDTH:DELTA_WIDTH + POOL_WIDTH, :]))
    h_ref[...] = h
    xn = _rms(h, nw_ref[...]).astype(BF16)
    xn_ref[...] = xn
    logits_t = _dot_nt(rwt_ref[...], xn)
    e_t = jnp.exp(logits_t - jnp.max(logits_t, axis=0, keepdims=True))
    pt_ref[...] = e_t / jnp.sum(e_t, axis=0, keepdims=True)


def _mixout(u, od, x, pw, ps, wout, nw, rw, tm=512):
    bsz, seq, _ = x.shape
    rwt = rw.T
    full = lambda a: pl.BlockSpec(a.shape, lambda b, i: (0,) * a.ndim)
    tile = lambda n: pl.BlockSpec((None, tm, n), lambda b, i: (b, i, 0))
    return pl.pallas_call(
        _mixout_kernel,
        grid=(bsz, seq // tm),
        in_specs=[pl.BlockSpec((None, seq, POOL_WIDTH), lambda b, i: (b, 0, 0)),
                  tile(DELTA_WIDTH), tile(D_MODEL), full(pw), full(ps), full(wout), full(nw), full(rwt)],
        out_specs=[tile(D_MODEL), tile(D_MODEL),
                   pl.BlockSpec((None, N_EXPERTS, tm), lambda b, i: (b, 0, i))],
        out_shape=[jax.ShapeDtypeStruct((bsz, seq, D_MODEL), F32),
                   jax.ShapeDtypeStruct((bsz, seq, D_MODEL), BF16),
                   jax.ShapeDtypeStruct((bsz, N_EXPERTS, seq), F32)],
        scratch_shapes=[pltpu.VMEM((tm + 2 * POOL_PAD, POOL_WIDTH), F32)],
        compiler_params=_params(("arbitrary", "arbitrary")),
        name="mixout",
    )(u, od, x, pw, ps, wout, nw, rwt)


def _route_kernel(pt_ref, slott_ref, slot_ref, cnt_ref, *, cap):
    p = pt_ref[...]
    rows, seq = p.shape
    n_e = slot_ref.shape[2]

    def bracket(mid_fn):
        def step(_, bounds):
            lo, hi = bounds
            mid = mid_fn(lo, hi)
            enough = jnp.sum((p >= mid).astype(F32), axis=1, keepdims=True) >= cap
            return jnp.where(enough, mid, lo), jnp.where(enough, hi, mid)
        return step

    bounds = (jnp.zeros((rows, 1), F32), jnp.full((rows, 1), 2.0, F32))
    bounds = lax.fori_loop(0, N_BISECT_GEO, bracket(lambda lo, hi: jnp.sqrt(jnp.maximum(lo, F32_TINY) * hi)),
                           bounds)
    lo, hi = lax.fori_loop(0, N_BISECT_LIN, bracket(lambda lo, hi: 0.5 * (lo + hi)), bounds)

    above = p >= hi
    tied = (p >= lo) & (p < hi)
    need = cap - jnp.sum(above.astype(F32), axis=1, keepdims=True)
    ri = lax.broadcasted_iota(I32, (CHUNK, CHUNK), 0)
    ci = lax.broadcasted_iota(I32, (CHUNK, CHUNK), 1)
    tri_b = (ri <= ci).astype(BF16)

    def prefix_count(mask):
        carry = jnp.zeros((rows, 1), F32)
        blocks = []
        for c0 in range(0, seq, CHUNK):
            part = _dot(mask[:, c0:c0 + CHUNK].astype(BF16), tri_b) + carry
            blocks.append(part)
            carry = part[:, CHUNK - 1:CHUNK]
        return jnp.concatenate(blocks, axis=1)

    tied_f = tied.astype(F32)
    sel = above | (tied & (prefix_count(tied_f) - tied_f < need))
    pos = prefix_count(sel.astype(F32))
    slot_t = jnp.where(sel, pos - 1.0, -1.0)
    slott_ref[...] = slot_t.astype(I32)
    tt = seq // cnt_ref.shape[1]
    cnt_ref[...] = jnp.concatenate([pos[:, t1 - 1:t1] for t1 in range(tt, seq + 1, tt)], axis=1).astype(I32)
    eye_b = (lax.broadcasted_iota(I32, (n_e, n_e), 0) == lax.broadcasted_iota(I32, (n_e, n_e), 1)).astype(BF16)
    for b in range(rows // n_e):
        slot_ref[b] = _dot_tn(slot_t[b * n_e:(b + 1) * n_e, :].astype(BF16), eye_b).astype(I32)


def _route(probs_t, cap):
    bsz, n_e, seq = probs_t.shape
    rows = bsz * n_e
    n_tiles = seq // COMBINE_TILE
    slot_t, slot, tile_cnt = pl.pallas_call(
        functools.partial(_route_kernel, cap=cap),
        grid=(1,),
        in_specs=[pl.BlockSpec((rows, seq), lambda i: (0, 0))],
        out_specs=[pl.BlockSpec((rows, seq), lambda i: (0, 0)),
                   pl.BlockSpec((bsz, seq, n_e), lambda i: (0, 0, 0)),
                   pl.BlockSpec((rows, n_tiles), lambda i: (0, 0))],
        out_shape=[jax.ShapeDtypeStruct((rows, seq), I32),
                   jax.ShapeDtypeStruct((bsz, seq, n_e), I32),
                   jax.ShapeDtypeStruct((rows, n_tiles), I32)],
        compiler_params=_params(("arbitrary",)),
        name="route",
    )(probs_t.reshape(rows, seq))
    return slot_t.reshape(bsz, n_e, seq), slot, tile_cnt


def _slot_window(cnt_ref, row, tile, cap):
    hi = cnt_ref[row, tile]
    lo = jnp.where(tile > 0, cnt_ref[row, jnp.maximum(tile - 1, 0)], 0)
    base = jnp.minimum(lo - (lo & (BF16_ROWS - 1)), cap - SLOT_WINDOW)
    return base, hi - base > SLOT_WINDOW


def _gather_kernel(cnt_ref, slott_ref, pt_ref, xn_ref, xg_ref, gate_ref):
    n_e, cap = xg_ref.shape[0], xg_ref.shape[1]
    b, step = pl.program_id(0), pl.program_id(1)
    tt = COMBINE_TILE

    @pl.when(step == 0)
    def _():
        xg_ref[...] = jnp.zeros(xg_ref.shape, BF16)
        gate_ref[...] = jnp.zeros(gate_ref.shape, F32)

    packed = PACK * SLOT_WINDOW
    row_in = (lax.broadcasted_iota(I32, (packed, 1), 0) & (SLOT_WINDOW - 1)).astype(F32)
    overflows = []
    for sub in range(TILES_PER_STEP):
        tile = step * TILES_PER_STEP + sub
        toks = slice(sub * tt, (sub + 1) * tt)
        slot_t = slott_ref[:, toks].astype(F32)
        probs_t = pt_ref[:, toks]
        xn = xn_ref[toks, :]
        for group in range(n_e // PACK):
            targets, gates, bases = [], [], []
            for j in range(PACK):
                e = group * PACK + j
                base, overflow = _slot_window(cnt_ref, b * n_e + e, tile, cap)
                base_f = base.astype(F32)
                end_f = jnp.where(overflow, base_f, base_f + SLOT_WINDOW)
                srow = slot_t[e:e + 1, :]
                local = jnp.where((srow >= base_f) & (srow < end_f), srow - base_f, -1.0)
                targets.append(jnp.broadcast_to(local, (SLOT_WINDOW, tt)))
                gates.append(jnp.broadcast_to(probs_t[e:e + 1, :], (SLOT_WINDOW, tt)))
                bases.append(base)
                overflows.append(overflow)
            hit = jnp.concatenate(targets, axis=0) == row_in
            rows = _dot(hit.astype(BF16), xn)
            gate = jnp.sum(jnp.where(hit, jnp.concatenate(gates, axis=0), 0.0), axis=1, keepdims=True)
            for j in range(PACK):
                e = group * PACK + j
                win = pl.ds(pl.multiple_of(bases[j], BF16_ROWS), SLOT_WINDOW)
                part = slice(j * SLOT_WINDOW, (j + 1) * SLOT_WINDOW)
                xg_ref[e, win, :] = (xg_ref[e, win, :].astype(F32) + rows[part]).astype(BF16)
                gate_ref[e, win, :] += gate[part]
    all_slots = lax.broadcasted_iota(I32, (cap, 1), 0).astype(F32)
    for sub in range(TILES_PER_STEP):
        toks = slice(sub * tt, (sub + 1) * tt)
        for e in range(n_e):

            @pl.when(overflows[sub * n_e + e])
            def _():
                hit = slott_ref[e:e + 1, toks].astype(F32) == all_slots
                xg_ref[e] = (xg_ref[e].astype(F32) + _dot(hit.astype(BF16), xn_ref[toks, :])).astype(BF16)
                gate_ref[e] += jnp.sum(jnp.where(hit, pt_ref[e:e + 1, toks], 0.0), axis=1, keepdims=True)


def _gather(tile_cnt, slot_t, probs_t, xn, cap):
    bsz, n_e, seq = slot_t.shape
    tt = COMBINE_TILE * TILES_PER_STEP
    rows = pl.BlockSpec((None, n_e, tt), lambda b, t, cnt: (b, 0, t))
    return pl.pallas_call(
        _gather_kernel,
        grid_spec=pltpu.PrefetchScalarGridSpec(
            num_scalar_prefetch=1,
            grid=(bsz, seq // tt),
            in_specs=[rows, rows, pl.BlockSpec((None, tt, D_MODEL), lambda b, t, cnt: (b, t, 0))],
            out_specs=[pl.BlockSpec((n_e, None, cap, D_MODEL), lambda b, t, cnt: (0, b, 0, 0)),
                       pl.BlockSpec((n_e, None, cap, 1), lambda b, t, cnt: (0, b, 0, 0))]),
        out_shape=[jax.ShapeDtypeStruct((n_e, bsz, cap, D_MODEL), BF16),
                   jax.ShapeDtypeStruct((n_e, bsz, cap, 1), F32)],
        compiler_params=_params(("arbitrary", "arbitrary")),
        name="gather",
    )(tile_cnt, slot_t, probs_t, xn)


def _ffn_kernel(xg_ref, gate_ref, wg_ref, wu_ref, wd_ref, y_ref, acc_ref, wgb, wub, wdb, *, tr):
    f = pl.program_id(1)
    last = pl.num_programs(1) - 1
    n_blocks = xg_ref.shape[0] // tr

    def gate_up(r):
        x = xg_ref[r * tr:(r + 1) * tr, :]
        return _dot(x, wgb[...]), _dot(x, wub[...])

    def ff_tile(first, final):
        wgb[...] = wg_ref[...].astype(BF16)
        gate0 = _dot(xg_ref[0:tr, :], wgb[...])
        wub[...] = wu_ref[...].astype(BF16)
        nxt = (gate0, _dot(xg_ref[0:tr, :], wub[...]))
        wdb[...] = wd_ref[...].astype(BF16)
        for r in range(n_blocks):
            gate, up = nxt
            if r + 1 < n_blocks:
                nxt = gate_up(r + 1)
            rows = slice(r * tr, (r + 1) * tr)
            hidden = (gate * _sigmoid(gate) * up).astype(BF16)
            total = _dot(hidden, wdb[...])
            if not first:
                total = total + acc_ref[rows, :]
            if final:
                y_ref[rows, :] = (total * gate_ref[rows, :]).astype(BF16)
            else:
                acc_ref[rows, :] = total

    pl.when(f == 0)(functools.partial(ff_tile, True, False))
    pl.when((f > 0) & (f < last))(functools.partial(ff_tile, False, False))
    pl.when(f == last)(functools.partial(ff_tile, False, True))


def _ffn(xg, gates, wg, wu, wd, tf=512, tr=256):
    n_e, rows, _ = xg.shape
    ff = wg.shape[2]
    return pl.pallas_call(
        functools.partial(_ffn_kernel, tr=tr),
        grid=(n_e, ff // tf),
        in_specs=[pl.BlockSpec((None, rows, D_MODEL), lambda e, f: (e, 0, 0)),
                  pl.BlockSpec((None, rows, 1), lambda e, f: (e, 0, 0)),
                  pl.BlockSpec((None, D_MODEL, tf), lambda e, f: (e, 0, f)),
                  pl.BlockSpec((None, D_MODEL, tf), lambda e, f: (e, 0, f)),
                  pl.BlockSpec((None, tf, D_MODEL), lambda e, f: (e, f, 0))],
        out_specs=pl.BlockSpec((None, rows, D_MODEL), lambda e, f: (e, 0, 0)),
        out_shape=jax.ShapeDtypeStruct((n_e, rows, D_MODEL), BF16),
        scratch_shapes=[pltpu.VMEM((rows, D_MODEL), F32),
                        pltpu.VMEM((D_MODEL, tf), BF16),
                        pltpu.VMEM((D_MODEL, tf), BF16),
                        pltpu.VMEM((tf, D_MODEL), BF16)],
        compiler_params=_params(("arbitrary", "arbitrary")),
        name="ffn",
    )(xg, gates, wg, wu, wd)


def _combine_kernel(cnt_ref, slot_ref, y_ref, h_ref, nw_ref, out_ref):
    n_e, cap = y_ref.shape[0], y_ref.shape[1]
    b, step = pl.program_id(0), pl.program_id(1)
    tt = COMBINE_TILE
    packed = PACK * SLOT_WINDOW
    lane = lax.broadcasted_iota(I32, (1, packed), 1)
    lane_f = lane.astype(F32)
    all_slots = lax.broadcasted_iota(I32, (1, cap), 1).astype(F32)
    overflows = []
    for sub in range(TILES_PER_STEP):
        tile = step * TILES_PER_STEP + sub
        toks = slice(sub * tt, (sub + 1) * tt)
        slot = slot_ref[toks, :].astype(F32)
        acc = h_ref[toks, :]
        for group in range(n_e // PACK):
            target = jnp.full((1, packed), -1.0, F32)
            windows = []
            for j in range(PACK):
                e = group * PACK + j
                base, overflow = _slot_window(cnt_ref, b * n_e + e, tile, cap)
                base_f = base.astype(F32)
                end_f = jnp.where(overflow, base_f, base_f + SLOT_WINDOW)
                col = slot[:, e:e + 1]
                inside = (col >= base_f) & (col < end_f)
                local = jnp.where(inside, col - base_f + j * SLOT_WINDOW, -1.0)
                target = jnp.where((lane >> (SLOT_WINDOW.bit_length() - 1)) == j, local, target)
                windows.append(y_ref[e, pl.ds(pl.multiple_of(base, BF16_ROWS), SLOT_WINDOW), :])
                overflows.append(overflow)
            onehot = (target == lane_f).astype(BF16)
            acc = acc + _dot(onehot, jnp.concatenate(windows, axis=0))
        out_ref[toks, :] = acc
    for sub in range(TILES_PER_STEP):
        toks = slice(sub * tt, (sub + 1) * tt)
        for e in range(n_e):

            @pl.when(overflows[sub * n_e + e])
            def _():
                full = (slot_ref[toks, e:e + 1].astype(F32) == all_slots).astype(BF16)
                out_ref[toks, :] += _dot(full, y_ref[e])

    out_ref[...] = _rms(out_ref[...], nw_ref[...])


def _combine(tile_cnt, slot, y, h, nw):
    bsz, seq, n_e = slot.shape
    cap = y.shape[2]
    tt = COMBINE_TILE * TILES_PER_STEP
    tok = lambda n: pl.BlockSpec((None, tt, n), lambda b, t, cnt: (b, t, 0))
    return pl.pallas_call(
        _combine_kernel,
        grid_spec=pltpu.PrefetchScalarGridSpec(
            num_scalar_prefetch=1,
            grid=(bsz, seq // tt),
            in_specs=[tok(n_e),
                      pl.BlockSpec((n_e, None, cap, D_MODEL), lambda b, t, cnt: (0, b, 0, 0)),
                      tok(D_MODEL),
                      pl.BlockSpec(nw.shape, lambda b, t, cnt: (0, 0))],
            out_specs=tok(D_MODEL)),
        out_shape=jax.ShapeDtypeStruct((bsz, seq, D_MODEL), F32),
        compiler_params=_params(("arbitrary", "arbitrary")),
        name="combine",
    )(tile_cnt, slot, y, h, nw)


def _layer(x, norm_mix_w, w_in, conv_w, a_log_fwd, dt_bias_fwd, a_log_bwd, dt_bias_bwd,
           head_norm_w, pool_w, pool_scale, w_out, norm_ffn_w, router_w,
           expert_w_gate, expert_w_up, expert_w_down, norm_final_w):
    bsz, seq, _ = x.shape
    nch = seq // CHUNK
    cap = EC_CAPACITY * seq // N_EXPERTS
    c_z = 3 * DELTA_WIDTH
    c_ab = c_z + DELTA_WIDTH
    c_u = c_ab + N_AB
    w_bf = w_in.astype(BF16)
    q, k, v, z, u, abt = _inproj(
        x, norm_mix_w.reshape(1, D_MODEL),
        w_bf[:, :c_z], w_bf[:, c_z:c_ab], w_bf[:, c_u:], w_bf[:, c_ab:c_u].T, conv_w)

    gate_params = jnp.stack([a_log_fwd, dt_bias_fwd, a_log_bwd, dt_bias_bwd]).astype(F32)
    o_delta = _delta(gate_params, q, k, v, z,
                     abt.reshape(bsz, N_AB, nch, CHUNK), head_norm_w.reshape(1, HEAD_DIM))

    h, xn, probs_t = _mixout(u, o_delta, x,
                             pool_w.astype(BF16), pool_scale.reshape(1, POOL_WIDTH),
                             w_out.astype(BF16), norm_ffn_w.reshape(1, D_MODEL), router_w.astype(BF16))
    slot_t, slot, tile_cnt = _route(probs_t, cap)
    xg, gates = _gather(tile_cnt, slot_t, probs_t, xn, cap)
    y = _ffn(xg.reshape(N_EXPERTS, bsz * cap, D_MODEL), gates.reshape(N_EXPERTS, bsz * cap, 1),
             expert_w_gate, expert_w_up, expert_w_down)
    return _combine(tile_cnt, slot, y.reshape(N_EXPERTS, bsz, cap, D_MODEL), h,
                    norm_final_w.reshape(1, D_MODEL))


def kernel(x, norm_mix_w, w_in, conv_w, a_log_fwd, dt_bias_fwd, a_log_bwd, dt_bias_bwd,
           head_norm_w, pool_w, pool_scale, w_out, norm_ffn_w, router_w,
           expert_w_gate, expert_w_up, expert_w_down, norm_final_w):
    assert w_in.shape[0] == 1, "single-layer stack: the final norm is fused into the layer's last kernel"
    first = lambda a: a.reshape(a.shape[1:])
    return _layer(x, first(norm_mix_w), first(w_in), first(conv_w), first(a_log_fwd), first(dt_bias_fwd),
                  first(a_log_bwd), first(dt_bias_bwd), first(head_norm_w), first(pool_w),
                  first(pool_scale), first(w_out), first(norm_ffn_w), first(router_w),
                  first(expert_w_gate), first(expert_w_up), first(expert_w_down), norm_final_w)
```

```python
import functools

import jax
import jax.numpy as jnp
from jax import lax
from jax.experimental import pallas as pl
from jax.experimental.pallas import tpu as pltpu

F32 = jnp.float32
BF16 = jnp.bfloat16
I32 = jnp.int32

D_MODEL = 1024
N_HEADS = 4
HEAD_DIM = 128
DELTA_WIDTH = N_HEADS * HEAD_DIM
POOL_WINDOWS = (2, 4, 8, 16)
POOL_GROUP_DIM = 128
POOL_WIDTH = len(POOL_WINDOWS) * POOL_GROUP_DIM
SHORT_CONV = 5
N_EXPERTS = 16
EC_CAPACITY = 2
EXPERT_FF = 2 * D_MODEL
RMS_EPS = 1e-6

MXU_COLS = 256
SUBLANES = 8
CHUNK = 128
N_BISECT_GEO = 34
N_BISECT_LIN = 6
F32_TINY = 2.0 ** -126
BF16_ROWS = 16
COMBINE_TILE = 256
TILES_PER_STEP = 4
SLOT_WINDOW = 64
PACK = 4
HALO = 16
POOL_PAD = 16
N_AB = 4 * N_HEADS
N_COLQ = 4
N_COLQ_PAD = 8
VMEM_LIMIT = 56 * 1024 * 1024


def _sigmoid(x):
    return 1.0 / (1.0 + jnp.exp(-x))


def _softplus(x):
    return jnp.maximum(x, 0.0) + jnp.log(1.0 + jnp.exp(-jnp.abs(x)))


def _dot(a, b):
    return jnp.dot(a, b, preferred_element_type=F32)


def _dot_nt(a, b):
    return lax.dot_general(a, b, (((1,), (1,)), ((), ())), preferred_element_type=F32)


def _dot_tn(a, b):
    return lax.dot_general(a, b, (((0,), (0,)), ((), ())), preferred_element_type=F32)


def _rms(x, w):
    return x * lax.rsqrt(jnp.mean(x * x, axis=-1, keepdims=True) + RMS_EPS) * w


def _params(sem):
    return pltpu.CompilerParams(dimension_semantics=sem, vmem_limit_bytes=VMEM_LIMIT)


def _inproj_kernel(x_ref, xprev_ref, xnext_ref, nw_ref, wqkv_ref, wz_ref, wu_ref, wabt_ref, cw_ref,
                   q_ref, k_ref, v_ref, z_ref, u_ref, abt_ref):
    tm = x_ref.shape[0]
    i = pl.program_id(1)
    nw = nw_ref[...]
    x_ext = jnp.concatenate([xprev_ref[...], x_ref[...], xnext_ref[...]], axis=0)
    xn_ext = _rms(x_ext, nw).astype(BF16)
    xn = xn_ext[HALO:HALO + tm, :]
    n_ext = tm + 2 * HALO
    keep_before = jnp.where(i > 0, 1.0, 0.0)
    keep_after = jnp.where(i < pl.num_programs(1) - 1, 1.0, 0.0)

    def project(pair):
        cols = slice(pair * MXU_COLS, (pair + 1) * MXU_COLS)
        res = _dot(xn_ext, wqkv_ref[:, cols])
        return jnp.concatenate([res[0:HALO] * keep_before, res[HALO:HALO + tm],
                                res[HALO + tm:n_ext] * keep_after], axis=0)

    def conv_act(blk, ext):
        cols = slice(blk * HEAD_DIM, (blk + 1) * HEAD_DIM)
        acc = None
        for j in range(SHORT_CONV):
            shift = (SHORT_CONV // 2 - j) % n_ext
            rolled = ext if shift == 0 else pltpu.roll(ext, shift, 0)
            term = rolled[HALO:HALO + tm] * cw_ref[j:j + 1, cols]
            acc = term if acc is None else acc + term
        act = acc * _sigmoid(acc)
        kind, head = divmod(blk, N_HEADS)
        if kind == 0:
            scale = lax.rsqrt(jnp.sum(act * act, axis=-1, keepdims=True) + RMS_EPS) * (HEAD_DIM ** -0.5)
            q_ref[head] = (act * scale).astype(BF16)
        elif kind == 1:
            scale = lax.rsqrt(jnp.sum(act * act, axis=-1, keepdims=True) + RMS_EPS)
            k_ref[head] = (act * scale).astype(BF16)
        else:
            v_ref[head] = act.astype(BF16)

    n_pairs = 3 * DELTA_WIDTH // MXU_COLS
    heads_per_pair = MXU_COLS // HEAD_DIM
    nxt = project(0)
    for pair in range(n_pairs):
        cur = nxt
        if pair + 1 < n_pairs:
            nxt = project(pair + 1)
        else:
            z = _dot(xn, wz_ref[...])
            for head in range(N_HEADS):
                z_ref[head] = z[:, head * HEAD_DIM:(head + 1) * HEAD_DIM].astype(BF16)
            u_ref[...] = _dot(xn, wu_ref[...]).astype(BF16)
            abt_ref[...] = _dot_nt(wabt_ref[...], xn)
        for sub in range(heads_per_pair):
            conv_act(pair * heads_per_pair + sub, cur[:, sub * HEAD_DIM:(sub + 1) * HEAD_DIM])


def _inproj(x, nw, wqkv, wz, wu, wabt, conv_w, tm=512):
    bsz, seq, _ = x.shape
    full = lambda a: pl.BlockSpec(a.shape, lambda b, i: (0, 0))
    heads = pl.BlockSpec((None, N_HEADS, tm, HEAD_DIM), lambda b, i: (b, 0, i, 0))
    halo_per_tile = tm // HALO
    head_shape = jax.ShapeDtypeStruct((bsz, N_HEADS, seq, HEAD_DIM), BF16)
    return pl.pallas_call(
        _inproj_kernel,
        grid=(bsz, seq // tm),
        in_specs=[pl.BlockSpec((None, tm, D_MODEL), lambda b, i: (b, i, 0)),
                  pl.BlockSpec((None, HALO, D_MODEL),
                               lambda b, i: (b, jnp.maximum(i * halo_per_tile - 1, 0), 0)),
                  pl.BlockSpec((None, HALO, D_MODEL),
                               lambda b, i: (b, jnp.minimum((i + 1) * halo_per_tile, seq // HALO - 1), 0)),
                  full(nw), full(wqkv), full(wz), full(wu), full(wabt), full(conv_w)],
        out_specs=[heads, heads, heads, heads,
                   pl.BlockSpec((None, tm, POOL_WIDTH), lambda b, i: (b, i, 0)),
                   pl.BlockSpec((None, N_AB, tm), lambda b, i: (b, 0, i))],
        out_shape=[head_shape, head_shape, head_shape, head_shape,
                   jax.ShapeDtypeStruct((bsz, seq, POOL_WIDTH), BF16),
                   jax.ShapeDtypeStruct((bsz, N_AB, seq), F32)],
        compiler_params=_params(("arbitrary", "arbitrary")),
        name="inproj",
    )(x, x, x, nw, wqkv, wz, wu, wabt, conv_w)


def _block_rows(t, size, parity):
    n = t.shape[0]
    return jnp.concatenate([t[r:r + size] for r in range(parity * size, n, 2 * size)], axis=0)


def _merge_block_rows(t, new_rows, size, parity):
    n = t.shape[0]
    parts = []
    for i, r in enumerate(range(0, n, 2 * size)):
        new = new_rows[i * size:(i + 1) * size]
        old = t[r + (1 - parity) * size:r + (2 - parity) * size]
        parts += [old, new] if parity else [new, old]
    return jnp.concatenate(parts, axis=0)


def _delta_kernel(gp_ref, q_ref, k_ref, v_ref, z_ref, abt_ref, hnw_ref, out_ref,
                  rowb, lvl, sm_s, sn_s, oq_s, o_s):
    n_heads, seq = q_ref.shape[0], q_ref.shape[1]
    nch = seq // CHUNK
    n_lvl = lvl.shape[0]
    ri = lax.broadcasted_iota(I32, (CHUNK, CHUNK), 0)
    ci = lax.broadcasted_iota(I32, (CHUNK, CHUNK), 1)

    def gate_rows(h):
        for d in range(2):
            a_row = abt_ref[2 * d * N_HEADS + h]
            b_row = abt_ref[(2 * d + 1) * N_HEADS + h]
            a_log = jnp.full((1, CHUNK), gp_ref[2 * d, h], F32)
            g_row = -jnp.exp(a_log) * _softplus(a_row + gp_ref[2 * d + 1, h])
            tri = (ri <= ci) if d == 0 else (ri >= ci)
            gc_row = jnp.dot(g_row, tri.astype(F32), precision=lax.Precision.HIGHEST,
                             preferred_element_type=F32)
            g_last = gc_row[:, CHUNK - 1:CHUNK] if d == 0 else gc_row[:, 0:1]
            g_last = jnp.broadcast_to(g_last, (nch, CHUNK))
            rowb[d, 0] = gc_row
            rowb[d, 1] = g_last
            rowb[d, 2] = jnp.exp(g_last)
            rowb[d, 3] = _sigmoid(b_row)

    sel_r = lax.broadcasted_iota(I32, (2 * N_COLQ_PAD, N_COLQ * CHUNK), 0)
    sel_c = lax.broadcasted_iota(I32, (2 * N_COLQ_PAD, N_COLQ * CHUNK), 1)
    col_sel = ((sel_r & (N_COLQ_PAD - 1)) == (sel_c >> (CHUNK.bit_length() - 1))).astype(BF16)
    pad_rows = jnp.zeros((N_COLQ_PAD - N_COLQ, CHUNK), F32)

    for l in range(n_lvl):
        same_pair = (ri >> (l + 1)) == (ci >> (l + 1))
        other_half = (ri >> l) != (ci >> l)
        lvl[l] = (same_pair & other_half).astype(F32)
    eye = (ri == ci).astype(F32)

    def prep_head(h, carry):
        gate_rows(h)
        chains = []
        for c in range(nch):
            r0 = c * CHUNK
            qb = q_ref[h, r0:r0 + CHUNK, :]
            kb = k_ref[h, r0:r0 + CHUNK, :]
            q = qb.astype(F32)
            k = kb.astype(F32)
            v = v_ref[h, r0:r0 + CHUNK, :].astype(F32)
            kk = _dot_nt(kb, kb)
            qk = _dot_nt(qb, kb)
            rows = jnp.concatenate(
                [rowb[0, 0, pl.ds(c, 1), :], rowb[0, 3, pl.ds(c, 1), :],
                 rowb[1, 0, pl.ds(c, 1), :], rowb[1, 3, pl.ds(c, 1), :], pad_rows], axis=0)
            rows_hi = rows.astype(BF16).astype(F32)
            split = jnp.concatenate([rows_hi, rows - rows_hi], axis=0).astype(BF16)
            cols = _dot_tn(split, col_sel)
            for d in range(2):
                incl = (ri >= ci) if d == 0 else (ri <= ci)
                strict = (ri > ci) if d == 0 else (ri < ci)
                gcol = cols[:, 2 * d * CHUNK:(2 * d + 1) * CHUNK]
                grow = rowb[d, 0, pl.ds(c, 1), :]
                beta = cols[:, (2 * d + 1) * CHUNK:(2 * d + 2) * CHUNK]
                decay = jnp.where(incl, jnp.exp(jnp.where(incl, gcol - grow, 0.0)), 0.0)
                a_mat = jnp.where(strict, beta * kk * decay, 0.0)
                egc = jnp.exp(gcol)
                g_last = rowb[d, 1, pl.ds(c, 1), :]
                chains.append(dict(
                    c=c, d=d, r0=r0, a=a_mat,
                    rhs=jnp.concatenate([k * (beta * egc), v * beta], axis=1).astype(BF16),
                    qkm=jnp.where(incl, qk * decay, 0.0).astype(BF16),
                    kd=(k * jnp.exp(g_last - gcol)).astype(BF16),
                    qd=q * egc))
        t_inv = [eye - ch["a"] * lvl[0] for ch in chains]
        for l in range(1, n_lvl):
            size = 1 << l
            t_b = [t.astype(BF16) for t in t_inv]
            c_b = [(ch["a"] * lvl[l]).astype(BF16) for ch in chains]
            if size < SUBLANES:
                xs = [_dot(tb, cb) for tb, cb in zip(t_b, c_b)]
                ys = [_dot(x.astype(BF16), tb) for x, tb in zip(xs, t_b)]
                t_inv = [t - y for t, y in zip(t_inv, ys)]
            else:
                halves = [_block_rows(t, size, 1 - ch["d"]) for t, ch in zip(t_inv, chains)]
                xs = [_dot(hf.astype(BF16), cb) for hf, cb in zip(halves, c_b)]
                ys = [_dot(x.astype(BF16), tb) for x, tb in zip(xs, t_b)]
                t_inv = [_merge_block_rows(t, hf - y, size, 1 - ch["d"])
                         for t, hf, y, ch in zip(t_inv, halves, ys, chains)]
        wus = [_dot(t.astype(BF16), ch["rhs"]).astype(BF16) for t, ch in zip(t_inv, chains)]
        kwus = [_dot_tn(ch["kd"], wu) for ch, wu in zip(chains, wus)]
        qwus = [_dot(ch["qkm"], wu) for ch, wu in zip(chains, wus)]
        for ch, kwu, qwu in zip(chains, kwus, qwus):
            c, d, r0 = ch["c"], ch["d"], ch["r0"]
            gamma = rowb[d, 2, pl.ds(c, 1), :]
            sm_s[h, d, c] = (eye * gamma - kwu[:, :HEAD_DIM]).astype(BF16)
            sn_s[h, d, c] = kwu[:, HEAD_DIM:].astype(BF16)
            oq_s[h, d, r0:r0 + CHUNK, :] = (ch["qd"] - qwu[:, :HEAD_DIM]).astype(BF16)
        for c in range(nch):
            o_s[h, c * CHUNK:(c + 1) * CHUNK, :] = qwus[2 * c][:, HEAD_DIM:] + qwus[2 * c + 1][:, HEAD_DIM:]
        return carry

    lax.fori_loop(0, n_heads, prep_head, 0)

    def scan_step(i, states):
        new_states = []
        for h in range(n_heads):
            for d in range(2):
                c = i if d == 0 else nch - 1 - i
                r0 = pl.multiple_of(c * CHUNK, CHUNK)
                s_b = states[2 * h + d].astype(BF16)
                o_s[h, pl.ds(r0, CHUNK), :] += _dot(oq_s[h, d, pl.ds(r0, CHUNK), :], s_b)
                new_states.append(sn_s[h, d, c].astype(F32) + _dot(sm_s[h, d, c], s_b))
        return tuple(new_states)

    zero_state = jnp.zeros((HEAD_DIM, HEAD_DIM), F32)
    lax.fori_loop(0, nch, scan_step, (zero_state,) * (2 * n_heads), unroll=2)

    for h in range(n_heads):
        o = o_s[h]
        zg = z_ref[h].astype(F32)
        out_ref[:, h * HEAD_DIM:(h + 1) * HEAD_DIM] = (
            _rms(o, hnw_ref[...]) * (zg * _sigmoid(zg))).astype(out_ref.dtype)


def _delta(gate_params, q, k, v, z, abt, hnw):
    bsz, n_heads, seq, _ = q.shape
    nch = seq // CHUNK
    n_lvl = CHUNK.bit_length() - 1
    heads = pl.BlockSpec((None, n_heads, seq, HEAD_DIM), lambda b: (b, 0, 0, 0))
    return pl.pallas_call(
        _delta_kernel,
        grid=(bsz,),
        in_specs=[pl.BlockSpec(memory_space=pltpu.SMEM),
                  heads, heads, heads, heads,
                  pl.BlockSpec((None, N_AB, nch, CHUNK), lambda b: (b, 0, 0, 0)),
                  pl.BlockSpec(hnw.shape, lambda b: (0, 0))],
        out_specs=pl.BlockSpec((None, seq, n_heads * HEAD_DIM), lambda b: (b, 0, 0)),
        out_shape=jax.ShapeDtypeStruct((bsz, seq, n_heads * HEAD_DIM), BF16),
        scratch_shapes=[
            pltpu.VMEM((2, N_COLQ, nch, CHUNK), F32),
            pltpu.VMEM((n_lvl, CHUNK, CHUNK), F32),
            pltpu.VMEM((n_heads, 2, nch, HEAD_DIM, HEAD_DIM), BF16),
            pltpu.VMEM((n_heads, 2, nch, HEAD_DIM, HEAD_DIM), BF16),
            pltpu.VMEM((n_heads, 2, seq, HEAD_DIM), BF16),
            pltpu.VMEM((n_heads, seq, HEAD_DIM), F32),
        ],
        compiler_params=_params(("arbitrary",)),
        name="delta",
    )(gate_params, q, k, v, z, abt, hnw)


def _mixout_kernel(u_ref, od_ref, x_ref, pw_ref, ps_ref, wout_ref, nw_ref, rwt_ref,
                   h_ref, xn_ref, pt_ref, upad):
    seq = u_ref.shape[0]
    tm = od_ref.shape[0]
    i = pl.program_id(1)
    n_tiles = pl.num_programs(1)
    t0 = pl.multiple_of(i * tm, tm)

    upad[POOL_PAD:POOL_PAD + tm, :] = u_ref[pl.ds(t0, tm), :].astype(F32)
    before = u_ref[pl.ds(pl.multiple_of(jnp.maximum(t0 - POOL_PAD, 0), POOL_PAD), POOL_PAD), :].astype(F32)
    after = u_ref[pl.ds(pl.multiple_of(jnp.minimum(t0 + tm, seq - POOL_PAD), POOL_PAD), POOL_PAD), :].astype(F32)
    upad[0:POOL_PAD, :] = jnp.where(i > 0, before, 0.0)
    upad[POOL_PAD + tm:POOL_PAD + tm + POOL_PAD, :] = jnp.where(i < n_tiles - 1, after, 0.0)

    tglob = t0 + lax.broadcasted_iota(I32, (tm, 1), 0)
    pooled_out = []
    for g, window in enumerate(POOL_WINDOWS):
        lo = window // 2
        hi = window - lo - 1
        cols = slice(g * POOL_GROUP_DIM, (g + 1) * POOL_GROUP_DIM)
        total = None
        for dlt in range(-lo, hi + 1):
            term = upad[POOL_PAD + dlt:POOL_PAD + dlt + tm, cols]
            total = term if total is None else total + term
        count = (jnp.minimum(tglob + hi + 1, seq) - jnp.maximum(tglob - lo, 0)).astype(F32)
        diff = total / count - upad[POOL_PAD:POOL_PAD + tm, cols]
        pooled_out.append(_dot(diff.astype(BF16), pw_ref[g]))
    o_pool = jnp.concatenate(pooled_out, axis=1) * ps_ref[...]

    h = (x_ref[...]
         + _dot(od_ref[...], wout_ref[0:DELTA_WIDTH, :])
         + _dot(o_pool.astype(BF16), wout_ref[DELTA_WIDTH:DELTA_WIDTH + POOL_WIDTH, :]))
    h_ref[...] = h
    xn = _rms(h, nw_ref[...]).astype(BF16)
    xn_ref[...] = xn
    logits_t = _dot_nt(rwt_ref[...], xn)
    e_t = jnp.exp(logits_t - jnp.max(logits_t, axis=0, keepdims=True))
    pt_ref[...] = e_t / jnp.sum(e_t, axis=0, keepdims=True)


def _mixout(u, od, x, pw, ps, wout, nw, rw, tm=512):
    bsz, seq, _ = x.shape
    rwt = rw.T
    full = lambda a: pl.BlockSpec(a.shape, lambda b, i: (0,) * a.ndim)
    tile = lambda n: pl.BlockSpec((None, tm, n), lambda b, i: (b, i, 0))
    return pl.pallas_call(
        _mixout_kernel,
        grid=(bsz, seq // tm),
        in_specs=[pl.BlockSpec((None, seq, POOL_WIDTH), lambda b, i: (b, 0, 0)),
                  tile(DELTA_WIDTH), tile(D_MODEL), full(pw), full(ps), full(wout), full(nw), full(rwt)],
        out_specs=[tile(D_MODEL), tile(D_MODEL),
                   pl.BlockSpec((None, N_EXPERTS, tm), lambda b, i: (b, 0, i))],
        out_shape=[jax.ShapeDtypeStruct((bsz, seq, D_MODEL), F32),
                   jax.ShapeDtypeStruct((bsz, seq, D_MODEL), BF16),
                   jax.ShapeDtypeStruct((bsz, N_EXPERTS, seq), F32)],
        scratch_shapes=[pltpu.VMEM((tm + 2 * POOL_PAD, POOL_WIDTH), F32)],
        compiler_params=_params(("arbitrary", "arbitrary")),
        name="mixout",
    )(u, od, x, pw, ps, wout, nw, rwt)


def _route_kernel(pt_ref, slott_ref, slot_ref, cnt_ref, *, cap):
    p = pt_ref[...]
    rows, seq = p.shape
    n_e = slot_ref.shape[2]

    def bracket(mid_fn):
        def step(_, bounds):
            lo, hi = bounds
            mid = mid_fn(lo, hi)
            enough = jnp.sum((p >= mid).astype(F32), axis=1, keepdims=True) >= cap
            return jnp.where(enough, mid, lo), jnp.where(enough, hi, mid)
        return step

    bounds = (jnp.zeros((rows, 1), F32), jnp.full((rows, 1), 2.0, F32))
    bounds = lax.fori_loop(0, N_BISECT_GEO, bracket(lambda lo, hi: jnp.sqrt(jnp.maximum(lo, F32_TINY) * hi)),
                           bounds)
    lo, hi = lax.fori_loop(0, N_BISECT_LIN, bracket(lambda lo, hi: 0.5 * (lo + hi)), bounds)

    above = p >= hi
    tied = (p >= lo) & (p < hi)
    need = cap - jnp.sum(above.astype(F32), axis=1, keepdims=True)
    ri = lax.broadcasted_iota(I32, (CHUNK, CHUNK), 0)
    ci = lax.broadcasted_iota(I32, (CHUNK, CHUNK), 1)
    tri_b = (ri <= ci).astype(BF16)

    def prefix_count(mask):
        carry = jnp.zeros((rows, 1), F32)
        blocks = []
        for c0 in range(0, seq, CHUNK):
            part = _dot(mask[:, c0:c0 + CHUNK].astype(BF16), tri_b) + carry
            blocks.append(part)
            carry = part[:, CHUNK - 1:CHUNK]
        return jnp.concatenate(blocks, axis=1)

    tied_f = tied.astype(F32)
    sel = above | (tied & (prefix_count(tied_f) - tied_f < need))
    pos = prefix_count(sel.astype(F32))
    slot_t = jnp.where(sel, pos - 1.0, -1.0)
    slott_ref[...] = slot_t.astype(I32)
    tt = seq // cnt_ref.shape[1]
    cnt_ref[...] = jnp.concatenate([pos[:, t1 - 1:t1] for t1 in range(tt, seq + 1, tt)], axis=1).astype(I32)
    eye_b = (lax.broadcasted_iota(I32, (n_e, n_e), 0) == lax.broadcasted_iota(I32, (n_e, n_e), 1)).astype(BF16)
    for b in range(rows // n_e):
        slot_ref[b] = _dot_tn(slot_t[b * n_e:(b + 1) * n_e, :].astype(BF16), eye_b).astype(I32)


def _route(probs_t, cap):
    bsz, n_e, seq = probs_t.shape
    rows = bsz * n_e
    n_tiles = seq // COMBINE_TILE
    slot_t, slot, tile_cnt = pl.pallas_call(
        functools.partial(_route_kernel, cap=cap),
        grid=(1,),
        in_specs=[pl.BlockSpec((rows, seq), lambda i: (0, 0))],
        out_specs=[pl.BlockSpec((rows, seq), lambda i: (0, 0)),
                   pl.BlockSpec((bsz, seq, n_e), lambda i: (0, 0, 0)),
                   pl.BlockSpec((rows, n_tiles), lambda i: (0, 0))],
        out_shape=[jax.ShapeDtypeStruct((rows, seq), I32),
                   jax.ShapeDtypeStruct((bsz, seq, n_e), I32),
                   jax.ShapeDtypeStruct((rows, n_tiles), I32)],
        compiler_params=_params(("arbitrary",)),
        name="route",
    )(probs_t.reshape(rows, seq))
    return slot_t.reshape(bsz, n_e, seq), slot, tile_cnt


def _slot_window(cnt_ref, row, tile, cap):
    hi = cnt_ref[row, tile]
    lo = jnp.where(tile > 0, cnt_ref[row, jnp.maximum(tile - 1, 0)], 0)
    base = jnp.minimum(lo - (lo & (BF16_ROWS - 1)), cap - SLOT_WINDOW)
    return base, hi - base > SLOT_WINDOW


def _gather_kernel(cnt_ref, slott_ref, pt_ref, xn_ref, xg_ref, gate_ref):
    n_e, cap = xg_ref.shape[0], xg_ref.shape[1]
    b, step = pl.program_id(0), pl.program_id(1)
    tt = COMBINE_TILE

    @pl.when(step == 0)
    def _():
        xg_ref[...] = jnp.zeros(xg_ref.shape, BF16)
        gate_ref[...] = jnp.zeros(gate_ref.shape, F32)

    packed = PACK * SLOT_WINDOW
    row_in = (lax.broadcasted_iota(I32, (packed, 1), 0) & (SLOT_WINDOW - 1)).astype(F32)
    overflows = []
    for sub in range(TILES_PER_STEP):
        tile = step * TILES_PER_STEP + sub
        toks = slice(sub * tt, (sub + 1) * tt)
        slot_t = slott_ref[:, toks].astype(F32)
        probs_t = pt_ref[:, toks]
        xn = xn_ref[toks, :]
        for group in range(n_e // PACK):
            targets, gates, bases = [], [], []
            for j in range(PACK):
                e = group * PACK + j
                base, overflow = _slot_window(cnt_ref, b * n_e + e, tile, cap)
                base_f = base.astype(F32)
                end_f = jnp.where(overflow, base_f, base_f + SLOT_WINDOW)
                srow = slot_t[e:e + 1, :]
                local = jnp.where((srow >= base_f) & (srow < end_f), srow - base_f, -1.0)
                targets.append(jnp.broadcast_to(local, (SLOT_WINDOW, tt)))
                gates.append(jnp.broadcast_to(probs_t[e:e + 1, :], (SLOT_WINDOW, tt)))
                bases.append(base)
                overflows.append(overflow)
            hit = jnp.concatenate(targets, axis=0) == row_in
            rows = _dot(hit.astype(BF16), xn)
            gate = jnp.sum(jnp.where(hit, jnp.concatenate(gates, axis=0), 0.0), axis=1, keepdims=True)
            for j in range(PACK):
                e = group * PACK + j
                win = pl.ds(pl.multiple_of(bases[j], BF16_ROWS), SLOT_WINDOW)
                part = slice(j * SLOT_WINDOW, (j + 1) * SLOT_WINDOW)
                xg_ref[e, win, :] = (xg_ref[e, win, :].astype(F32) + rows[part]).astype(BF16)
                gate_ref[e, win, :] += gate[part]
    all_slots = lax.broadcasted_iota(I32, (cap, 1), 0).astype(F32)
    for sub in range(TILES_PER_STEP):
        toks = slice(sub * tt, (sub + 1) * tt)
        for e in range(n_e):

            @pl.when(overflows[sub * n_e + e])
            def _():
                hit = slott_ref[e:e + 1, toks].astype(F32) == all_slots
                xg_ref[e] = (xg_ref[e].astype(F32) + _dot(hit.astype(BF16), xn_ref[toks, :])).astype(BF16)
                gate_ref[e] += jnp.sum(jnp.where(hit, pt_ref[e:e + 1, toks], 0.0), axis=1, keepdims=True)


def _gather(tile_cnt, slot_t, probs_t, xn, cap):
    bsz, n_e, seq = slot_t.shape
    tt = COMBINE_TILE * TILES_PER_STEP
    rows = pl.BlockSpec((None, n_e, tt), lambda b, t, cnt: (b, 0, t))
    return pl.pallas_call(
        _gather_kernel,
        grid_spec=pltpu.PrefetchScalarGridSpec(
            num_scalar_prefetch=1,
            grid=(bsz, seq // tt),
            in_specs=[rows, rows, pl.BlockSpec((None, tt, D_MODEL), lambda b, t, cnt: (b, t, 0))],
            out_specs=[pl.BlockSpec((n_e, None, cap, D_MODEL), lambda b, t, cnt: (0, b, 0, 0)),
                       pl.BlockSpec((n_e, None, cap, 1), lambda b, t, cnt: (0, b, 0, 0))]),
        out_shape=[jax.ShapeDtypeStruct((n_e, bsz, cap, D_MODEL), BF16),
                   jax.ShapeDtypeStruct((n_e, bsz, cap, 1), F32)],
        compiler_params=_params(("arbitrary", "arbitrary")),
        name="gather",
    )(tile_cnt, slot_t, probs_t, xn)


def _ffn_kernel(xg_ref, gate_ref, wg_ref, wu_ref, wd_ref, y_ref, acc_ref, wgb, wub, wdb, *, tr):
    f = pl.program_id(1)
    last = pl.num_programs(1) - 1
    n_blocks = xg_ref.shape[0] // tr

    def gate_up(r):
        x = xg_ref[r * tr:(r + 1) * tr, :]
        return _dot(x, wgb[...]), _dot(x, wub[...])

    def ff_tile(first, final):
        wgb[...] = wg_ref[...].astype(BF16)
        gate0 = _dot(xg_ref[0:tr, :], wgb[...])
        wub[...] = wu_ref[...].astype(BF16)
        nxt = (gate0, _dot(xg_ref[0:tr, :], wub[...]))
        wdb[...] = wd_ref[...].astype(BF16)
        for r in range(n_blocks):
            gate, up = nxt
            if r + 1 < n_blocks:
                nxt = gate_up(r + 1)
            rows = slice(r * tr, (r + 1) * tr)
            hidden = (gate * _sigmoid(gate) * up).astype(BF16)
            total = _dot(hidden, wdb[...])
            if not first:
                total = total + acc_ref[rows, :]
            if final:
                y_ref[rows, :] = (total * gate_ref[rows, :]).astype(BF16)
            else:
                acc_ref[rows, :] = total

    pl.when(f == 0)(functools.partial(ff_tile, True, False))
    pl.when((f > 0) & (f < last))(functools.partial(ff_tile, False, False))
    pl.when(f == last)(functools.partial(ff_tile, False, True))


def _ffn(xg, gates, wg, wu, wd, tf=512, tr=512):
    n_e, rows, _ = xg.shape
    ff = wg.shape[2]
    return pl.pallas_call(
        functools.partial(_ffn_kernel, tr=tr),
        grid=(n_e, ff // tf),
        in_specs=[pl.BlockSpec((None, rows, D_MODEL), lambda e, f: (e, 0, 0)),
                  pl.BlockSpec((None, rows, 1), lambda e, f: (e, 0, 0)),
                  pl.BlockSpec((None, D_MODEL, tf), lambda e, f: (e, 0, f)),
                  pl.BlockSpec((None, D_MODEL, tf), lambda e, f: (e, 0, f)),
                  pl.BlockSpec((None, tf, D_MODEL), lambda e, f: (e, f, 0))],
        out_specs=pl.BlockSpec((None, rows, D_MODEL), lambda e, f: (e, 0, 0)),
        out_shape=jax.ShapeDtypeStruct((n_e, rows, D_MODEL), BF16),
        scratch_shapes=[pltpu.VMEM((rows, D_MODEL), F32),
                        pltpu.VMEM((D_MODEL, tf), BF16),
                        pltpu.VMEM((D_MODEL, tf), BF16),
                        pltpu.VMEM((tf, D_MODEL), BF16)],
        compiler_params=_params(("arbitrary", "arbitrary")),
        name="ffn",
    )(xg, gates, wg, wu, wd)


def _combine_kernel(cnt_ref, slot_ref, y_ref, h_ref, nw_ref, out_ref):
    n_e, cap = y_ref.shape[0], y_ref.shape[1]
    b, step = pl.program_id(0), pl.program_id(1)
    tt = COMBINE_TILE
    packed = PACK * SLOT_WINDOW
    lane = lax.broadcasted_iota(I32, (1, packed), 1)
    lane_f = lane.astype(F32)
    all_slots = lax.broadcasted_iota(I32, (1, cap), 1).astype(F32)
    overflows = []
    for sub in range(TILES_PER_STEP):
        tile = step * TILES_PER_STEP + sub
        toks = slice(sub * tt, (sub + 1) * tt)
        slot = slot_ref[toks, :].astype(F32)
        acc = h_ref[toks, :]
        for group in range(n_e // PACK):
            target = jnp.full((1, packed), -1.0, F32)
            windows = []
            for j in range(PACK):
                e = group * PACK + j
                base, overflow = _slot_window(cnt_ref, b * n_e + e, tile, cap)
                base_f = base.astype(F32)
                end_f = jnp.where(overflow, base_f, base_f + SLOT_WINDOW)
                col = slot[:, e:e + 1]
                inside = (col >= base_f) & (col < end_f)
                local = jnp.where(inside, col - base_f + j * SLOT_WINDOW, -1.0)
                target = jnp.where((lane >> (SLOT_WINDOW.bit_length() - 1)) == j, local, target)
                windows.append(y_ref[e, pl.ds(pl.multiple_of(base, BF16_ROWS), SLOT_WINDOW), :])
                overflows.append(overflow)
            onehot = (target == lane_f).astype(BF16)
            acc = acc + _dot(onehot, jnp.concatenate(windows, axis=0))
        out_ref[toks, :] = acc
    for sub in range(TILES_PER_STEP):
        toks = slice(sub * tt, (sub + 1) * tt)
        for e in range(n_e):

            @pl.when(overflows[sub * n_e + e])
            def _():
                full = (slot_ref[toks, e:e + 1].astype(F32) == all_slots).astype(BF16)
                out_ref[toks, :] += _dot(full, y_ref[e])

    out_ref[...] = _rms(out_ref[...], nw_ref[...])


def _combine(tile_cnt, slot, y, h, nw):
    bsz, seq, n_e = slot.shape
    cap = y.shape[2]
    tt = COMBINE_TILE * TILES_PER_STEP
    tok = lambda n: pl.BlockSpec((None, tt, n), lambda b, t, cnt: (b, t, 0))
    return pl.pallas_call(
        _combine_kernel,
        grid_spec=pltpu.PrefetchScalarGridSpec(
            num_scalar_prefetch=1,
            grid=(bsz, seq // tt),
            in_specs=[tok(n_e),
                      pl.BlockSpec((n_e, None, cap, D_MODEL), lambda b, t, cnt: (0, b, 0, 0)),
                      tok(D_MODEL),
                      pl.BlockSpec(nw.shape, lambda b, t, cnt: (0, 0))],
            out_specs=tok(D_MODEL)),
        out_shape=jax.ShapeDtypeStruct((bsz, seq, D_MODEL), F32),
        compiler_params=_params(("arbitrary", "arbitrary")),
        name="combine",
    )(tile_cnt, slot, y, h, nw)


def _layer(x, norm_mix_w, w_in, conv_w, a_log_fwd, dt_bias_fwd, a_log_bwd, dt_bias_bwd,
           head_norm_w, pool_w, pool_scale, w_out, norm_ffn_w, router_w,
           expert_w_gate, expert_w_up, expert_w_down, norm_final_w):
    bsz, seq, _ = x.shape
    nch = seq // CHUNK
    cap = EC_CAPACITY * seq // N_EXPERTS
    c_z = 3 * DELTA_WIDTH
    c_ab = c_z + DELTA_WIDTH
    c_u = c_ab + N_AB
    w_bf = w_in.astype(BF16)
    q, k, v, z, u, abt = _inproj(
        x, norm_mix_w.reshape(1, D_MODEL),
        w_bf[:, :c_z], w_bf[:, c_z:c_ab], w_bf[:, c_u:], w_bf[:, c_ab:c_u].T, conv_w)

    gate_params = jnp.stack([a_log_fwd, dt_bias_fwd, a_log_bwd, dt_bias_bwd]).astype(F32)
    o_delta = _delta(gate_params, q, k, v, z,
                     abt.reshape(bsz, N_AB, nch, CHUNK), head_norm_w.reshape(1, HEAD_DIM))

    h, xn, probs_t = _mixout(u, o_delta, x,
                             pool_w.astype(BF16), pool_scale.reshape(1, POOL_WIDTH),
                             w_out.astype(BF16), norm_ffn_w.reshape(1, D_MODEL), router_w.astype(BF16))
    slot_t, slot, tile_cnt = _route(probs_t, cap)
    xg, gates = _gather(tile_cnt, slot_t, probs_t, xn, cap)
    y = _ffn(xg.reshape(N_EXPERTS, bsz * cap, D_MODEL), gates.reshape(N_EXPERTS, bsz * cap, 1),
             expert_w_gate, expert_w_up, expert_w_down)
    return _combine(tile_cnt, slot, y.reshape(N_EXPERTS, bsz, cap, D_MODEL), h,
                    norm_final_w.reshape(1, D_MODEL))


def kernel(x, norm_mix_w, w_in, conv_w, a_log_fwd, dt_bias_fwd, a_log_bwd, dt_bias_bwd,
           head_norm_w, pool_w, pool_scale, w_out, norm_ffn_w, router_w,
           expert_w_gate, expert_w_up, expert_w_down, norm_final_w):
    assert w_in.shape[0] == 1, "single-layer stack: the final norm is fused into the layer's last kernel"
    first = lambda a: a.reshape(a.shape[1:])
    return _layer(x, first(norm_mix_w), first(w_in), first(conv_w), first(a_log_fwd), first(dt_bias_fwd),
                  first(a_log_bwd), first(dt_bias_bwd), first(head_norm_w), first(pool_w),
                  first(pool_scale), first(w_out), first(norm_ffn_w), first(router_w),
                  first(expert_w_gate), first(expert_w_up), first(expert_w_down), norm_final_w)
```
